```python
import math
import jax, jax.numpy as jnp
from jax import lax
import numpy as np

D_MODEL = 1024
BATCH = 16
SEQ = 4096
DEPTH = 1
DEC_BATCH = 32
DEC_SEQ = 16
PAST_LEN = 2048

CHUNK = 64
HEAD_DIM = 64
N_HEADS = 8
N_KV_HEADS = 2
GROUP = N_HEADS // N_KV_HEADS
ATT_WIDTH = N_HEADS * HEAD_DIM
KV_WIDTH = N_KV_HEADS * HEAD_DIM
CONV_WIDTH = D_MODEL - ATT_WIDTH
CONV_K = 31
WINDOW = 128
N_PREV_CHUNKS = WINDOW // CHUNK
BAND = (N_PREV_CHUNKS + 1) * CHUNK
IN_WIDTH = ATT_WIDTH + 2 * KV_WIDTH + 2 * CONV_WIDTH
D_FF = 2816
NUM_BUCKETS = 32
MAX_DISTANCE = 128
EPS = 1e-6

kernel_name = "hymba_conformer_swa_stream_step"


def rmsnorm(x, g):
    xf = x.astype(jnp.float32)
    xf = xf * lax.rsqrt(jnp.mean(xf * xf, axis=-1, keepdims=True) + EPS)
    return (xf * g.astype(jnp.float32)).astype(x.dtype)


def layernorm(x, g, b):
    xf = x.astype(jnp.float32)
    mu = jnp.mean(xf, axis=-1, keepdims=True)
    xc = xf - mu
    var = jnp.mean(xc * xc, axis=-1, keepdims=True)
    return (xc * lax.rsqrt(var + EPS) * g.astype(jnp.float32) + b.astype(jnp.float32)).astype(x.dtype)


def swiglu_ffn(x, norm_g, w_gu, w_down):
    h = rmsnorm(x, norm_g) @ w_gu
    gate, up = jnp.split(h, 2, axis=-1)
    return (jax.nn.silu(gate) * up) @ w_down


def rel_bucket(rel):
    nb = NUM_BUCKETS // 2
    max_exact = nb // 2
    ret = jnp.where(rel > 0, nb, 0)
    n = jnp.abs(rel)
    nf = jnp.maximum(n, 1).astype(jnp.float32)
    large = max_exact + (jnp.log(nf / max_exact) / math.log(MAX_DISTANCE / max_exact)
                         * (nb - max_exact)).astype(jnp.int32)
    large = jnp.minimum(large, nb - 1)
    return ret + jnp.where(n < max_exact, n, large)


def rel_bias(table, qpos, kpos):
    b = rel_bucket(kpos[None, :] - qpos[:, None])
    return jnp.transpose(table[b], (2, 0, 1)).astype(jnp.float32)


def project(h, w_in, q_gain, k_gain):
    B, L, _ = h.shape
    z = h @ w_in
    q, k, v, a, g = jnp.split(
        z, [ATT_WIDTH, ATT_WIDTH + KV_WIDTH, ATT_WIDTH + 2 * KV_WIDTH,
            ATT_WIDTH + 2 * KV_WIDTH + CONV_WIDTH], axis=-1)
    q = rmsnorm(q.reshape(B, L, N_HEADS, HEAD_DIM), q_gain) * (HEAD_DIM ** -0.5)
    k = rmsnorm(k.reshape(B, L, N_KV_HEADS, HEAD_DIM), k_gain)
    v = v.reshape(B, L, N_KV_HEADS, HEAD_DIM)
    u = a * jax.nn.sigmoid(g)
    return q, k, v, u


def sink_attend(qb, kb, vb, bias, mask, sinks):
    LQ, LK = qb.shape[2], kb.shape[2]
    s = jnp.einsum('bnqkgd,bnskd->bnkgqs', qb, kb, preferred_element_type=jnp.float32)
    s = s + bias.reshape(N_KV_HEADS, GROUP, LQ, LK)
    s = jnp.where(mask[None, :, None, None, None, :], s, -jnp.inf)
    sink = jnp.broadcast_to(sinks.astype(jnp.float32).reshape(N_KV_HEADS, GROUP, 1, 1),
                            s.shape[:-1] + (1,))
    p = jax.nn.softmax(jnp.concatenate([s, sink], axis=-1), axis=-1)[..., :-1]
    return jnp.einsum('bnkgqs,bnskd->bnqkgd', p.astype(vb.dtype), vb)


def conv_tail(u_ext, conv_w, conv_b, ln_g, ln_b):
    y = lax.conv_general_dilated(u_ext, conv_w[:, None, :].astype(u_ext.dtype), (1,), 'VALID',
                                 dimension_numbers=('NWC', 'WIO', 'NWC'),
                                 feature_group_count=u_ext.shape[-1])
    y = layernorm(y + conv_b, ln_g, ln_b)
    return jax.nn.silu(y)


def setup_inputs(seed: int = 0) -> dict:
    key = jax.random.key(seed)
    ks = jax.random.split(key, 24)
    f32 = jnp.float32
    R = min(WINDOW, PAST_LEN)
    nrm = lambda k, shape, s: jax.random.normal(k, shape, f32) * s
    gain = lambda k, shape: 1.0 + 0.05 * jax.random.normal(k, shape, f32)
    return {
        "x_prompt": nrm(ks[0], (BATCH, SEQ, D_MODEL), 1.0),
        "x_sample": nrm(ks[1], (DEC_BATCH, DEC_SEQ, D_MODEL), 1.0),
        "cache_k": nrm(ks[2], (DEPTH, DEC_BATCH, R, N_KV_HEADS, HEAD_DIM), 1.0),
        "cache_v": nrm(ks[3], (DEPTH, DEC_BATCH, R, N_KV_HEADS, HEAD_DIM), 1.0),
        "state_conv": nrm(ks[4], (DEPTH, DEC_BATCH, CONV_K - 1, CONV_WIDTH), 0.5),
        "rel_bias_table": nrm(ks[5], (NUM_BUCKETS, N_HEADS), 0.5),
        "ffn1_norm": gain(ks[6], (DEPTH, D_MODEL)),
        "ffn1_w_gu": nrm(ks[7], (DEPTH, D_MODEL, 2 * D_FF), D_MODEL ** -0.5),
        "ffn1_w_down": nrm(ks[8], (DEPTH, D_FF, D_MODEL), D_FF ** -0.5),
        "mix_norm": gain(ks[9], (DEPTH, D_MODEL)),
        "w_in": nrm(ks[10], (DEPTH, D_MODEL, IN_WIDTH), D_MODEL ** -0.5),
        "q_norm": gain(ks[11], (DEPTH, HEAD_DIM)),
        "k_norm": gain(ks[12], (DEPTH, HEAD_DIM)),
        "sinks": nrm(ks[13], (DEPTH, N_HEADS), 0.5),
        "conv_w": nrm(ks[14], (DEPTH, CONV_K, CONV_WIDTH), CONV_K ** -0.5),
        "conv_b": nrm(ks[15], (DEPTH, CONV_WIDTH), 0.02),
        "conv_ln_g": gain(ks[16], (DEPTH, CONV_WIDTH)),
        "conv_ln_b": nrm(ks[17], (DEPTH, CONV_WIDTH), 0.02),
        "w_out": nrm(ks[18], (DEPTH, D_MODEL, D_MODEL), D_MODEL ** -0.5),
        "ffn2_norm": gain(ks[19], (DEPTH, D_MODEL)),
        "ffn2_w_gu": nrm(ks[20], (DEPTH, D_MODEL, 2 * D_FF), D_MODEL ** -0.5),
        "ffn2_w_down": nrm(ks[21], (DEPTH, D_FF, D_MODEL), D_FF ** -0.5),
        "final_norm": gain(ks[22], (DEPTH, D_MODEL)),
    }


def reference(x_prompt, x_sample, cache_k, cache_v, state_conv, rel_bias_table,
              ffn1_norm, ffn1_w_gu, ffn1_w_down, mix_norm, w_in, q_norm, k_norm, sinks,
              conv_w, conv_b, conv_ln_g, conv_ln_b, w_out, ffn2_norm, ffn2_w_gu,
              ffn2_w_down, final_norm):
    B, S, _ = x_prompt.shape
    DB, DS, _ = x_sample.shape
    R = cache_k.shape[2]
    RP = min(WINDOW, S)
    nc = S // CHUNK

    bias_p = rel_bias(rel_bias_table, jnp.arange(CHUNK), jnp.arange(BAND) - WINDOW)
    kabs = jnp.arange(nc)[:, None] * CHUNK - WINDOW + jnp.arange(BAND)[None, :]
    mask_p = kabs >= 0
    qpos_s = PAST_LEN + jnp.arange(DS)
    kpos_s = jnp.concatenate([PAST_LEN - R + jnp.arange(R), qpos_s])
    bias_s = rel_bias(rel_bias_table, qpos_s, kpos_s)
    mask_s = jnp.ones((1, R + DS), dtype=bool)

    xp, xs = x_prompt, x_sample
    nkp, nvp, ncp, nks, nvs, ncs = [], [], [], [], [], []
    for l in range(DEPTH):
        xp = xp + 0.5 * swiglu_ffn(xp, ffn1_norm[l], ffn1_w_gu[l], ffn1_w_down[l])
        xs = xs + 0.5 * swiglu_ffn(xs, ffn1_norm[l], ffn1_w_gu[l], ffn1_w_down[l])

        q, k, v, u = project(rmsnorm(xp, mix_norm[l]), w_in[l], q_norm[l], k_norm[l])
        pad = ((0, 0), (WINDOW, 0), (0, 0), (0, 0))
        kp = jnp.pad(k, pad).reshape(B, nc + N_PREV_CHUNKS, CHUNK, N_KV_HEADS, HEAD_DIM)
        vp = jnp.pad(v, pad).reshape(B, nc + N_PREV_CHUNKS, CHUNK, N_KV_HEADS, HEAD_DIM)
        kb = jnp.concatenate([kp[:, i:i + nc] for i in range(N_PREV_CHUNKS + 1)], axis=2)
        vb = jnp.concatenate([vp[:, i:i + nc] for i in range(N_PREV_CHUNKS + 1)], axis=2)
        qb = q.reshape(B, nc, CHUNK, N_KV_HEADS, GROUP, HEAD_DIM)
        att = sink_attend(qb, kb, vb, bias_p, mask_p, sinks[l]).reshape(B, S, ATT_WIDTH)
        u_ext = jnp.pad(u, ((0, 0), (CONV_K - 1, 0), (0, 0)))
        cy = conv_tail(u_ext, conv_w[l], conv_b[l], conv_ln_g[l], conv_ln_b[l])
        xp = xp + jnp.concatenate([att, cy], axis=-1) @ w_out[l]
        nkp.append(k[:, S - RP:])
        nvp.append(v[:, S - RP:])
        ncp.append(u[:, S - (CONV_K - 1):])

        q, k, v, u = project(rmsnorm(xs, mix_norm[l]), w_in[l], q_norm[l], k_norm[l])
        k_all = jnp.concatenate([cache_k[l].astype(k.dtype), k], axis=1)
        v_all = jnp.concatenate([cache_v[l].astype(v.dtype), v], axis=1)
        qb = q.reshape(DB, 1, DS, N_KV_HEADS, GROUP, HEAD_DIM)
        att = sink_attend(qb, k_all[:, None], v_all[:, None], bias_s, mask_s,
                          sinks[l]).reshape(DB, DS, ATT_WIDTH)
        u_ext = jnp.concatenate([state_conv[l].astype(u.dtype), u], axis=1)
        cy = conv_tail(u_ext, conv_w[l], conv_b[l], conv_ln_g[l], conv_ln_b[l])
        xs = xs + jnp.concatenate([att, cy], axis=-1) @ w_out[l]
        nks.append(k_all[:, DS:])
        nvs.append(v_all[:, DS:])
        ncs.append(u_ext[:, DS:])

        xp = xp + 0.5 * swiglu_ffn(xp, ffn2_norm[l], ffn2_w_gu[l], ffn2_w_down[l])
        xs = xs + 0.5 * swiglu_ffn(xs, ffn2_norm[l], ffn2_w_gu[l], ffn2_w_down[l])
        xp = rmsnorm(xp, final_norm[l])
        xs = rmsnorm(xs, final_norm[l])

    return (xp, xs, jnp.stack(nkp), jnp.stack(nvp), jnp.stack(ncp),
            jnp.stack(nks), jnp.stack(nvs), jnp.stack(ncs))
```

```python
import functools
import math

import numpy as np
import jax
import jax.numpy as jnp
from jax import lax
from jax.experimental import pallas as pl
from jax.experimental.pallas import tpu as pltpu

D_MODEL = 1024
CHUNK = 64
HEAD_DIM = 64
N_HEADS = 8
N_KV_HEADS = 2
GROUP = N_HEADS // N_KV_HEADS
ATT_WIDTH = N_HEADS * HEAD_DIM
KV_WIDTH = N_KV_HEADS * HEAD_DIM
CONV_WIDTH = D_MODEL - ATT_WIDTH
CONV_K = 31
WINDOW = 128
IN_WIDTH = ATT_WIDTH + 2 * KV_WIDTH + 2 * CONV_WIDTH
QK_WIDTH = ATT_WIDTH + KV_WIDTH
D_FF = 2816
NUM_BUCKETS = 32
MAX_DISTANCE = 128
EPS = 1e-6

LANES = 128
CONV_PAD = 32
FF_CHUNK = 1408
CONV_ROWS = 32
TOKEN_TILE = 512
MIX_TILE = 512
VMEM_LIMIT = 56 * 1024 * 1024

F32 = jnp.float32
BF16 = jnp.bfloat16


def _dot(a, b):
    return jnp.dot(a, b, preferred_element_type=F32)


def _rms(x, g):
    ms = jnp.mean(x * x, axis=-1, keepdims=True)
    return x * lax.rsqrt(ms + EPS) * g


def _swiglu(xn, wgu_ref, wd_ref):
    acc = None
    for c in range(D_FF // FF_CHUNK):
        lo = c * FF_CHUNK
        gate = _dot(xn, wgu_ref[:, lo:lo + FF_CHUNK])
        up = _dot(xn, wgu_ref[:, D_FF + lo:D_FF + lo + FF_CHUNK])
        act = (gate * jax.nn.sigmoid(gate) * up).astype(BF16)
        part = _dot(act, wd_ref[lo:lo + FF_CHUNK, :])
        acc = part if acc is None else acc + part
    return acc


def _ffn1_proj_kernel(x_ref, g1_ref, wgu_ref, wd_ref, gmix_ref, win_ref, gsum_ref, gqk_ref,
                      x1_ref, q_ref, k_ref, v_ref, u_ref):
    x = x_ref[...]
    x1 = x + 0.5 * _swiglu(_rms(x, g1_ref[...]).astype(BF16), wgu_ref, wd_ref)
    x1_ref[...] = x1
    z = _dot(_rms(x1, gmix_ref[...]).astype(BF16), win_ref[...])
    qk = z[:, :QK_WIDTH]
    sq = qk * qk
    hi = sq.astype(BF16)
    lo = (sq - hi.astype(F32)).astype(BF16)
    ss = _dot(hi, gsum_ref[...]) + _dot(lo, gsum_ref[...])
    qkn = qk * lax.rsqrt(ss * (1.0 / HEAD_DIM) + EPS) * gqk_ref[...]
    q_ref[...] = qkn[:, :ATT_WIDTH].astype(BF16)
    k_ref[...] = qkn[:, ATT_WIDTH:]
    v_ref[...] = z[:, QK_WIDTH:QK_WIDTH + KV_WIDTH]
    a = z[:, QK_WIDTH + KV_WIDTH:QK_WIDTH + KV_WIDTH + CONV_WIDTH]
    g = z[:, QK_WIDTH + KV_WIDTH + CONV_WIDTH:]
    u_ref[...] = a * jax.nn.sigmoid(g)


def _out_ffn2_kernel(x1_ref, mix_ref, wout_ref, g2_ref, wgu_ref, wd_ref, gfin_ref, y_ref):
    x2 = x1_ref[...] + _dot(mix_ref[...], wout_ref[...])
    x3 = x2 + 0.5 * _swiglu(_rms(x2, g2_ref[...]).astype(BF16), wgu_ref, wd_ref)
    y_ref[...] = _rms(x3, gfin_ref[...])


def _mixer_kernel(q_ref, kc_ref, kp_ref, vc_ref, vp_ref, uc_ref, up_ref, bias_ref, sink_ref,
                  cw_ref, cb_ref, lng_ref, lnb_ref, mix_ref, kw_ref, vw_ref, ue_ref,
                  *, chunk, tile, first_tile_has_no_past):
    i = pl.program_id(1)
    kw_ref[0:WINDOW, :] = kp_ref[...].astype(BF16)
    kw_ref[WINDOW:, :] = kc_ref[...].astype(BF16)
    vw_ref[0:WINDOW, :] = vp_ref[...].astype(BF16)
    vw_ref[WINDOW:, :] = vc_ref[...].astype(BF16)
    u_prev = up_ref[...]
    if first_tile_has_no_past:
        u_prev = jnp.where(i > 0, u_prev, 0.0)
    ue_ref[0:CONV_PAD, :] = u_prev
    ue_ref[CONV_PAD:, :] = uc_ref[...]

    kwin_len = WINDOW + chunk
    lane = lax.broadcasted_iota(jnp.int32, (chunk, LANES), 1)
    low_half = lane < HEAD_DIM
    sink = sink_ref[...]
    n_bias = bias_ref.shape[0]
    for c in range(tile // chunk):
        qc = q_ref[c * chunk:(c + 1) * chunk, :]
        zero = jnp.zeros_like(qc[:, :LANES])
        blocks = []
        for j in range(N_KV_HEADS):
            keep = low_half if j == 0 else jnp.logical_not(low_half)
            for p in range(GROUP):
                blocks.append(jnp.where(keep, qc[:, p * LANES:(p + 1) * LANES], zero))
        q_big = jnp.concatenate(blocks, axis=0)
        kwin = kw_ref[c * chunk:c * chunk + kwin_len, :]
        vwin = vw_ref[c * chunk:c * chunk + kwin_len, :]
        s = lax.dot_general(q_big, kwin, (((1,), (1,)), ((), ())), preferred_element_type=F32)
        bidx = jnp.minimum(i * (tile // chunk) + c, n_bias - 1)
        s = s + bias_ref[bidx]
        m = jnp.maximum(jnp.max(s, axis=-1, keepdims=True), sink)
        e = jnp.exp(s - m)
        denom = jnp.sum(e, axis=-1, keepdims=True) + jnp.exp(sink - m)
        o = _dot(e.astype(BF16), vwin) / denom
        for p in range(GROUP):
            o0 = o[p * chunk:(p + 1) * chunk, :]
            o1 = o[(GROUP + p) * chunk:(GROUP + p + 1) * chunk, :]
            mix_ref[c * chunk:(c + 1) * chunk, p * LANES:(p + 1) * LANES] = (
                jnp.where(low_half, o0, o1).astype(BF16))

    rows = min(CONV_ROWS, tile)
    first_tap = CONV_PAD - (CONV_K - 1)
    for r in range(tile // rows):
        acc = jnp.zeros((rows, CONV_WIDTH), F32)
        for j in range(CONV_K):
            lo = r * rows + first_tap + j
            acc = acc + ue_ref[lo:lo + rows, :] * cw_ref[j:j + 1, :]
        y = acc + cb_ref[...]
        mu = jnp.mean(y, axis=-1, keepdims=True)
        yc = y - mu
        var = jnp.mean(yc * yc, axis=-1, keepdims=True)
        yn = yc * lax.rsqrt(var + EPS) * lng_ref[...] + lnb_ref[...]
        mix_ref[r * rows:(r + 1) * rows, ATT_WIDTH:] = (yn * jax.nn.sigmoid(yn)).astype(BF16)


def _const_spec(shape):
    return pl.BlockSpec(shape, lambda *_: (0,) * len(shape), pipeline_mode=pl.Buffered(1))


def _params(n_axes):
    return pltpu.CompilerParams(dimension_semantics=("arbitrary",) * n_axes,
                                vmem_limit_bytes=VMEM_LIMIT)


def _ffn1_proj(x, g1, wgu, wd, gmix, win, gsum, gqk):
    n = x.shape[0]
    tm = min(TOKEN_TILE, n)
    row = lambda w: pl.BlockSpec((tm, w), lambda i: (i, 0))
    return pl.pallas_call(
        _ffn1_proj_kernel,
        grid=(n // tm,),
        in_specs=[row(D_MODEL), _const_spec(g1.shape), _const_spec(wgu.shape), _const_spec(wd.shape),
                  _const_spec(gmix.shape), _const_spec(win.shape), _const_spec(gsum.shape),
                  _const_spec(gqk.shape)],
        out_specs=[row(D_MODEL), row(ATT_WIDTH), row(KV_WIDTH), row(KV_WIDTH), row(CONV_WIDTH)],
        out_shape=[jax.ShapeDtypeStruct((n, D_MODEL), F32), jax.ShapeDtypeStruct((n, ATT_WIDTH), BF16),
                   jax.ShapeDtypeStruct((n, KV_WIDTH), F32), jax.ShapeDtypeStruct((n, KV_WIDTH), F32),
                   jax.ShapeDtypeStruct((n, CONV_WIDTH), F32)],
        compiler_params=_params(1),
        name="ffn1_proj",
    )(x, g1, wgu, wd, gmix, win, gsum, gqk)


def _out_ffn2(x1, mix, wout, g2, wgu, wd, gfin):
    n = x1.shape[0]
    tm = min(TOKEN_TILE, n)
    row = lambda w: pl.BlockSpec((tm, w), lambda i: (i, 0))
    return pl.pallas_call(
        _out_ffn2_kernel,
        grid=(n // tm,),
        in_specs=[row(D_MODEL), row(D_MODEL), _const_spec(wout.shape), _const_spec(g2.shape),
                  _const_spec(wgu.shape), _const_spec(wd.shape), _const_spec(gfin.shape)],
        out_specs=row(D_MODEL),
        out_shape=jax.ShapeDtypeStruct((n, D_MODEL), F32),
        compiler_params=_params(1),
        name="out_ffn2",
    )(x1, mix, wout, g2, wgu, wd, gfin)


def _mixer(q, k, v, u, k_past, v_past, u_past, bias, sink_col, cw, cb, lng, lnb,
           *, n_streams, seq, chunk, tile, past_is_own_stream):
    nt = seq // tile
    cur = lambda w: pl.BlockSpec((tile, w), lambda b, i: (b * nt + i, 0))
    if past_is_own_stream:
        kv_past = pl.BlockSpec(
            (WINDOW, KV_WIDTH), lambda b, i: (jnp.maximum((b * seq + i * tile) // WINDOW - 1, 0), 0))
        u_past_spec = pl.BlockSpec(
            (CONV_PAD, CONV_WIDTH), lambda b, i: (jnp.maximum((b * seq + i * tile) // CONV_PAD - 1, 0), 0))
    else:
        kv_past = pl.BlockSpec((WINDOW, KV_WIDTH), lambda b, i: (b, 0))
        u_past_spec = pl.BlockSpec((CONV_PAD, CONV_WIDTH), lambda b, i: (b, 0))
    body = functools.partial(_mixer_kernel, chunk=chunk, tile=tile,
                             first_tile_has_no_past=past_is_own_stream)
    return pl.pallas_call(
        body,
        grid=(n_streams, nt),
        in_specs=[cur(ATT_WIDTH), cur(KV_WIDTH), kv_past, cur(KV_WIDTH), kv_past, cur(CONV_WIDTH),
                  u_past_spec, _const_spec(bias.shape), _const_spec(sink_col.shape),
                  _const_spec(cw.shape), _const_spec(cb.shape), _const_spec(lng.shape),
                  _const_spec(lnb.shape)],
        out_specs=cur(D_MODEL),
        out_shape=jax.ShapeDtypeStruct((n_streams * seq, D_MODEL), BF16),
        scratch_shapes=[pltpu.VMEM((WINDOW + tile, KV_WIDTH), BF16),
                        pltpu.VMEM((WINDOW + tile, KV_WIDTH), BF16),
                        pltpu.VMEM((CONV_PAD + tile, CONV_WIDTH), F32)],
        compiler_params=_params(2),
        name="mixer",
    )(q, k, k_past, v, v_past, u, u_past, bias, sink_col, cw, cb, lng, lnb)


def _rel_bucket(rel):
    nb = NUM_BUCKETS // 2
    max_exact = nb // 2
    ret = np.where(rel > 0, nb, 0)
    n = np.abs(rel)
    nf = np.maximum(n, 1).astype(np.float32)
    large = max_exact + (np.log(nf / np.float32(max_exact)) / np.float32(math.log(MAX_DISTANCE / max_exact))
                         * np.float32(nb - max_exact)).astype(np.int32)
    large = np.minimum(large, nb - 1)
    return (ret + np.where(n < max_exact, n, large)).astype(np.int32)


def _bias_rows(table, qpos, kpos):
    bucket = _rel_bucket(kpos[None, :] - qpos[:, None])
    b = jnp.transpose(table[bucket], (2, 0, 1)).astype(F32)
    return b.reshape(N_HEADS * qpos.shape[0], kpos.shape[0])


def _head_pair_perm():
    cols = []
    for p in range(GROUP):
        for h in (p, GROUP + p):
            cols.extend(range(h * HEAD_DIM, (h + 1) * HEAD_DIM))
    return np.asarray(cols, dtype=np.int32)


def kernel(x_prompt, x_sample, cache_k, cache_v, state_conv, rel_bias_table, ffn1_norm, ffn1_w_gu,
           ffn1_w_down, mix_norm, w_in, q_norm, k_norm, sinks, conv_w, conv_b, conv_ln_g, conv_ln_b,
           w_out, ffn2_norm, ffn2_w_gu, ffn2_w_down, final_norm):
    B, S, _ = x_prompt.shape
    DB, DS, _ = x_sample.shape
    R = cache_k.shape[2]
    past_len = R
    assert ffn1_norm.shape[0] == 1 and R == WINDOW and S % MIX_TILE == 0 and DS % 8 == 0
    l = 0
    row = lambda a: a.reshape(1, -1).astype(F32)

    perm = _head_pair_perm()
    win = jnp.concatenate([w_in[l][:, perm], w_in[l][:, ATT_WIDTH:]], axis=1).astype(BF16)
    wout = jnp.concatenate([w_out[l][perm, :], w_out[l][ATT_WIDTH:, :]], axis=0).astype(BF16)
    gqk = jnp.concatenate([jnp.tile(q_norm[l], N_HEADS) * (HEAD_DIM ** -0.5),
                           jnp.tile(k_norm[l], N_KV_HEADS)]).reshape(1, QK_WIDTH).astype(F32)
    head_of_col = np.arange(QK_WIDTH) // HEAD_DIM
    gsum = jnp.asarray(head_of_col[:, None] == head_of_col[None, :], dtype=BF16)
    wgu1, wd1 = ffn1_w_gu[l].astype(BF16), ffn1_w_down[l].astype(BF16)
    wgu2, wd2 = ffn2_w_gu[l].astype(BF16), ffn2_w_down[l].astype(BF16)
    cw, cb = conv_w[l].astype(F32), row(conv_b[l])
    lng, lnb = row(conv_ln_g[l]), row(conv_ln_b[l])

    band = WINDOW + CHUNK
    bias_p = _bias_rows(rel_bias_table, np.arange(CHUNK), np.arange(band) - WINDOW)
    key = np.arange(band)[None, :]
    neg = lambda first_valid: jnp.where(key >= first_valid, bias_p, -jnp.inf)
    bias_prompt = jnp.stack([neg(WINDOW), neg(WINDOW - CHUNK), bias_p])
    qpos_s = past_len + np.arange(DS)
    kpos_s = np.concatenate([past_len - R + np.arange(R), qpos_s])
    bias_sample = _bias_rows(rel_bias_table, qpos_s, kpos_s)[None]
    sink_col = lambda n: jnp.repeat(sinks[l].astype(F32), n).reshape(N_HEADS * n, 1)

    def layer(x, n_streams, seq, chunk, tile, past):
        x1, q, k, v, u = _ffn1_proj(x, row(ffn1_norm[l]), wgu1, wd1, row(mix_norm[l]), win, gsum, gqk)
        if past is None:
            mix = _mixer(q, k, v, u, k, v, u, bias_prompt, sink_col(chunk), cw, cb, lng, lnb,
                         n_streams=n_streams, seq=seq, chunk=chunk, tile=tile, past_is_own_stream=True)
        else:
            mix = _mixer(q, k, v, u, *past, bias_sample, sink_col(chunk), cw, cb, lng, lnb,
                         n_streams=n_streams, seq=seq, chunk=chunk, tile=tile, past_is_own_stream=False)
        y = _out_ffn2(x1, mix, wout, row(ffn2_norm[l]), wgu2, wd2, row(final_norm[l]))
        return y, k, v, u

    yp, kp, vp, up = layer(x_prompt.reshape(B * S, D_MODEL), B, S, CHUNK, MIX_TILE, None)
    ck = cache_k[l].reshape(DB * R, KV_WIDTH).astype(F32)
    cv = cache_v[l].reshape(DB * R, KV_WIDTH).astype(F32)
    sc = jnp.pad(state_conv[l].astype(F32), ((0, 0), (CONV_PAD - (CONV_K - 1), 0), (0, 0)))
    ys, ks, vs, us = layer(x_sample.reshape(DB * DS, D_MODEL), DB, DS, DS, DS,
                           (ck, cv, sc.reshape(DB * CONV_PAD, CONV_WIDTH)))

    RP = min(WINDOW, S)
    heads = lambda a, n, t: a.reshape(n, t, N_KV_HEADS, HEAD_DIM)
    new_k_prompt = heads(kp, B, S)[:, S - RP:][None]
    new_v_prompt = heads(vp, B, S)[:, S - RP:][None]
    new_conv_prompt = up.reshape(B, S, CONV_WIDTH)[:, S - (CONV_K - 1):][None]
    new_k_sample = jnp.concatenate([cache_k[l].astype(F32), heads(ks, DB, DS)], axis=1)[:, DS:][None]
    new_v_sample = jnp.concatenate([cache_v[l].astype(F32), heads(vs, DB, DS)], axis=1)[:, DS:][None]
    new_conv_sample = jnp.concatenate([state_conv[l].astype(F32), us.reshape(DB, DS, CONV_WIDTH)],
                                      axis=1)[:, DS:][None]
    return (yp.reshape(B, S, D_MODEL), ys.reshape(DB, DS, D_MODEL), new_k_prompt, new_v_prompt,
            new_conv_prompt, new_k_sample, new_v_sample, new_conv_sample)
```

```python
import functools
import math

import numpy as np
import jax
import jax.numpy as jnp
from jax import lax
from jax.experimental import pallas as pl
from jax.experimental.pallas import tpu as pltpu

D_MODEL = 1024
CHUNK = 64
HEAD_DIM = 64
N_HEADS = 8
N_KV_HEADS = 2
GROUP = N_HEADS // N_KV_HEADS
ATT_WIDTH = N_HEADS * HEAD_DIM
KV_WIDTH = N_KV_HEADS * HEAD_DIM
CONV_WIDTH = D_MODEL - ATT_WIDTH
CONV_K = 31
WINDOW = 128
IN_WIDTH = ATT_WIDTH + 2 * KV_WIDTH + 2 * CONV_WIDTH
QK_WIDTH = ATT_WIDTH + KV_WIDTH
D_FF = 2816
NUM_BUCKETS = 32
MAX_DISTANCE = 128
EPS = 1e-6

LANES = 128
CONV_PAD = 32
FF_SPLITS = (0, 1536, D_FF)
SUBLANES = 8
NORM_GROUP = 256
CONV_ROWS = 32
TOKEN_TILE = 512
MIX_TILE = 512
VMEM_LIMIT = 56 * 1024 * 1024

F32 = jnp.float32
BF16 = jnp.bfloat16


def _dot(a, b):
    return jnp.dot(a, b, preferred_element_type=F32)


def _rms(x, g):
    ms = jnp.mean(x * x, axis=-1, keepdims=True)
    return x * lax.rsqrt(ms + EPS) * g


def _swiglu(xn, wgu_ref, wd_ref):
    acc = None
    for lo, hi in zip(FF_SPLITS[:-1], FF_SPLITS[1:]):
        gate = _dot(xn, wgu_ref[:, lo:hi])
        up = _dot(xn, wgu_ref[:, D_FF + lo:D_FF + hi])
        act = (gate * jax.nn.sigmoid(gate) * up).astype(BF16)
        part = _dot(act, wd_ref[lo:hi, :])
        acc = part if acc is None else acc + part
    return acc


def _ffn1_proj_kernel(x_ref, g1_ref, wgu_ref, wd_ref, gmix_ref, win_ref, gsum_ref, gqk_ref,
                      x1_ref, q_ref, k_ref, v_ref, u_ref):
    x = x_ref[...]
    x1 = x + 0.5 * _swiglu(_rms(x, g1_ref[...]).astype(BF16), wgu_ref, wd_ref)
    x1_ref[...] = x1
    z = _dot(_rms(x1, gmix_ref[...]).astype(BF16), win_ref[...])
    qk = z[:, :QK_WIDTH]
    sq = (qk * qk).astype(BF16)
    ss = jnp.concatenate(
        [_dot(sq[:, lo:lo + NORM_GROUP], gsum_ref[...]) for lo in range(0, ATT_WIDTH, NORM_GROUP)]
        + [_dot(sq[:, ATT_WIDTH:], gsum_ref[:KV_WIDTH, :KV_WIDTH])], axis=1)
    qkn = qk * lax.rsqrt(ss * (1.0 / HEAD_DIM) + EPS) * gqk_ref[...]
    q_ref[...] = qkn[:, :ATT_WIDTH].astype(BF16)
    k_ref[...] = qkn[:, ATT_WIDTH:]
    v_ref[...] = z[:, QK_WIDTH:QK_WIDTH + KV_WIDTH]
    a = z[:, QK_WIDTH + KV_WIDTH:QK_WIDTH + KV_WIDTH + CONV_WIDTH]
    g = z[:, QK_WIDTH + KV_WIDTH + CONV_WIDTH:]
    u_ref[...] = a * jax.nn.sigmoid(g)


def _out_ffn2_kernel(x1_ref, mix_ref, wout_ref, g2_ref, wgu_ref, wd_ref, gfin_ref, y_ref):
    x2 = x1_ref[...] + _dot(mix_ref[...], wout_ref[...])
    x3 = x2 + 0.5 * _swiglu(_rms(x2, g2_ref[...]).astype(BF16), wgu_ref, wd_ref)
    y_ref[...] = _rms(x3, gfin_ref[...])


def _mixer_kernel(q_ref, kc_ref, kp_ref, vc_ref, vp_ref, uc_ref, up_ref, bias_ref, sink_ref,
                  cw_ref, cb_ref, lng_ref, lnb_ref, mix_ref, kw_ref, vw_ref, ue_ref, us_ref,
                  *, chunk, tile, first_tile_has_no_past):
    i = pl.program_id(1)
    kw_ref[0:WINDOW, :] = kp_ref[...].astype(BF16)
    kw_ref[WINDOW:, :] = kc_ref[...].astype(BF16)
    vw_ref[0:WINDOW, :] = vp_ref[...].astype(BF16)
    vw_ref[WINDOW:, :] = vc_ref[...].astype(BF16)
    u_prev = up_ref[...]
    if first_tile_has_no_past:
        u_prev = jnp.where(i > 0, u_prev, 0.0)
    ue_ref[0:CONV_PAD, :] = u_prev
    ue_ref[CONV_PAD:, :] = uc_ref[...]

    kwin_len = WINDOW + chunk
    lane = lax.broadcasted_iota(jnp.int32, (chunk, LANES), 1)
    low_half = lane < HEAD_DIM
    sink = sink_ref[...]
    n_bias = bias_ref.shape[0]
    for c in range(tile // chunk):
        qc = q_ref[c * chunk:(c + 1) * chunk, :]
        zero = jnp.zeros_like(qc[:, :LANES])
        blocks = []
        for j in range(N_KV_HEADS):
            keep = low_half if j == 0 else jnp.logical_not(low_half)
            for p in range(GROUP):
                blocks.append(jnp.where(keep, qc[:, p * LANES:(p + 1) * LANES], zero))
        q_big = jnp.concatenate(blocks, axis=0)
        kwin = kw_ref[c * chunk:c * chunk + kwin_len, :]
        vwin = vw_ref[c * chunk:c * chunk + kwin_len, :]
        s = lax.dot_general(q_big, kwin, (((1,), (1,)), ((), ())), preferred_element_type=F32)
        bidx = jnp.minimum(i * (tile // chunk) + c, n_bias - 1)
        s = s + bias_ref[bidx]
        m = jnp.maximum(jnp.max(s, axis=-1, keepdims=True), sink)
        e = jnp.exp(s - m)
        denom = jnp.sum(e, axis=-1, keepdims=True) + jnp.exp(sink - m)
        o = _dot(e.astype(BF16), vwin) / denom
        for p in range(GROUP):
            o0 = o[p * chunk:(p + 1) * chunk, :]
            o1 = o[(GROUP + p) * chunk:(GROUP + p + 1) * chunk, :]
            mix_ref[c * chunk:(c + 1) * chunk, p * LANES:(p + 1) * LANES] = (
                jnp.where(low_half, o0, o1).astype(BF16))

    shifted_rows = us_ref.shape[1]
    for b in range(1, SUBLANES):
        us_ref[b - 1] = ue_ref[b:b + shifted_rows, :]
    rows = min(CONV_ROWS, tile)
    first_tap = CONV_PAD - (CONV_K - 1)
    for r in range(tile // rows):
        acc = jnp.zeros((rows, CONV_WIDTH), F32)
        for j in range(CONV_K):
            a, b = divmod(first_tap + j, SUBLANES)
            lo = r * rows + a * SUBLANES
            taps = ue_ref[lo:lo + rows, :] if b == 0 else us_ref[b - 1, lo:lo + rows, :]
            w = cw_ref[j * SUBLANES:(j + 1) * SUBLANES, :]
            acc = acc + taps * jnp.concatenate([w] * (rows // SUBLANES), axis=0)
        y = acc + cb_ref[...]
        mu = jnp.mean(y, axis=-1, keepdims=True)
        yc = y - mu
        var = jnp.mean(yc * yc, axis=-1, keepdims=True)
        yn = yc * lax.rsqrt(var + EPS) * lng_ref[...] + lnb_ref[...]
        mix_ref[r * rows:(r + 1) * rows, ATT_WIDTH:] = (yn * jax.nn.sigmoid(yn)).astype(BF16)


def _const_spec(shape):
    return pl.BlockSpec(shape, lambda *_: (0,) * len(shape), pipeline_mode=pl.Buffered(1))


def _params(n_axes):
    return pltpu.CompilerParams(dimension_semantics=("arbitrary",) * n_axes,
                                vmem_limit_bytes=VMEM_LIMIT)


def _ffn1_proj(x, g1, wgu, wd, gmix, win, gsum, gqk):
    n = x.shape[0]
    tm = min(TOKEN_TILE, n)
    row = lambda w: pl.BlockSpec((tm, w), lambda i: (i, 0))
    return pl.pallas_call(
        _ffn1_proj_kernel,
        grid=(n // tm,),
        in_specs=[row(D_MODEL), _const_spec(g1.shape), _const_spec(wgu.shape), _const_spec(wd.shape),
                  _const_spec(gmix.shape), _const_spec(win.shape), _const_spec(gsum.shape),
                  _const_spec(gqk.shape)],
        out_specs=[row(D_MODEL), row(ATT_WIDTH), row(KV_WIDTH), row(KV_WIDTH), row(CONV_WIDTH)],
        out_shape=[jax.ShapeDtypeStruct((n, D_MODEL), F32), jax.ShapeDtypeStruct((n, ATT_WIDTH), BF16),
                   jax.ShapeDtypeStruct((n, KV_WIDTH), F32), jax.ShapeDtypeStruct((n, KV_WIDTH), F32),
                   jax.ShapeDtypeStruct((n, CONV_WIDTH), F32)],
        compiler_params=_params(1),
        name="ffn1_proj",
    )(x, g1, wgu, wd, gmix, win, gsum, gqk)


def _out_ffn2(x1, mix, wout, g2, wgu, wd, gfin):
    n = x1.shape[0]
    tm = min(TOKEN_TILE, n)
    row = lambda w: pl.BlockSpec((tm, w), lambda i: (i, 0))
    return pl.pallas_call(
        _out_ffn2_kernel,
        grid=(n // tm,),
        in_specs=[row(D_MODEL), row(D_MODEL), _const_spec(wout.shape), _const_spec(g2.shape),
                  _const_spec(wgu.shape), _const_spec(wd.shape), _const_spec(gfin.shape)],
        out_specs=row(D_MODEL),
        out_shape=jax.ShapeDtypeStruct((n, D_MODEL), F32),
        compiler_params=_params(1),
        name="out_ffn2",
    )(x1, mix, wout, g2, wgu, wd, gfin)


def _mixer(q, k, v, u, k_past, v_past, u_past, bias, sink_col, cw, cb, lng, lnb,
           *, n_streams, seq, chunk, tile, past_is_own_stream):
    nt = seq // tile
    cur = lambda w: pl.BlockSpec((tile, w), lambda b, i: (b * nt + i, 0))
    if past_is_own_stream:
        kv_past = pl.BlockSpec(
            (WINDOW, KV_WIDTH), lambda b, i: (jnp.maximum((b * seq + i * tile) // WINDOW - 1, 0), 0))
        u_past_spec = pl.BlockSpec(
            (CONV_PAD, CONV_WIDTH), lambda b, i: (jnp.maximum((b * seq + i * tile) // CONV_PAD - 1, 0), 0))
    else:
        kv_past = pl.BlockSpec((WINDOW, KV_WIDTH), lambda b, i: (b, 0))
        u_past_spec = pl.BlockSpec((CONV_PAD, CONV_WIDTH), lambda b, i: (b, 0))
    body = functools.partial(_mixer_kernel, chunk=chunk, tile=tile,
                             first_tile_has_no_past=past_is_own_stream)
    return pl.pallas_call(
        body,
        grid=(n_streams, nt),
        in_specs=[cur(ATT_WIDTH), cur(KV_WIDTH), kv_past, cur(KV_WIDTH), kv_past, cur(CONV_WIDTH),
                  u_past_spec, _const_spec(bias.shape), _const_spec(sink_col.shape),
                  _const_spec(cw.shape), _const_spec(cb.shape), _const_spec(lng.shape),
                  _const_spec(lnb.shape)],
        out_specs=cur(D_MODEL),
        out_shape=jax.ShapeDtypeStruct((n_streams * seq, D_MODEL), BF16),
        scratch_shapes=[pltpu.VMEM((WINDOW + tile, KV_WIDTH), BF16),
                        pltpu.VMEM((WINDOW + tile, KV_WIDTH), BF16),
                        pltpu.VMEM((CONV_PAD + tile, CONV_WIDTH), F32),
                        pltpu.VMEM((SUBLANES - 1, CONV_PAD + tile - SUBLANES, CONV_WIDTH), F32)],
        compiler_params=_params(2),
        name="mixer",
    )(q, k, k_past, v, v_past, u, u_past, bias, sink_col, cw, cb, lng, lnb)


def _rel_bucket(rel):
    nb = NUM_BUCKETS // 2
    max_exact = nb // 2
    ret = np.where(rel > 0, nb, 0)
    n = np.abs(rel)
    nf = np.maximum(n, 1).astype(np.float32)
    large = max_exact + (np.log(nf / np.float32(max_exact)) / np.float32(math.log(MAX_DISTANCE / max_exact))
                         * np.float32(nb - max_exact)).astype(np.int32)
    large = np.minimum(large, nb - 1)
    return (ret + np.where(n < max_exact, n, large)).astype(np.int32)


def _bias_rows(table, qpos, kpos):
    bucket = _rel_bucket(kpos[None, :] - qpos[:, None])
    b = jnp.transpose(table[bucket], (2, 0, 1)).astype(F32)
    return b.reshape(N_HEADS * qpos.shape[0], kpos.shape[0])


def _head_pair_perm():
    cols = []
    for p in range(GROUP):
        for h in (p, GROUP + p):
            cols.extend(range(h * HEAD_DIM, (h + 1) * HEAD_DIM))
    return np.asarray(cols, dtype=np.int32)


def kernel(x_prompt, x_sample, cache_k, cache_v, state_conv, rel_bias_table, ffn1_norm, ffn1_w_gu,
           ffn1_w_down, mix_norm, w_in, q_norm, k_norm, sinks, conv_w, conv_b, conv_ln_g, conv_ln_b,
           w_out, ffn2_norm, ffn2_w_gu, ffn2_w_down, final_norm):
    B, S, _ = x_prompt.shape
    DB, DS, _ = x_sample.shape
    R = cache_k.shape[2]
    past_len = R
    assert ffn1_norm.shape[0] == 1 and R == WINDOW and S % MIX_TILE == 0 and DS % 8 == 0
    l = 0
    row = lambda a: a.reshape(1, -1).astype(F32)

    perm = _head_pair_perm()
    win = jnp.concatenate([w_in[l][:, perm], w_in[l][:, ATT_WIDTH:]], axis=1).astype(BF16)
    wout = jnp.concatenate([w_out[l][perm, :], w_out[l][ATT_WIDTH:, :]], axis=0).astype(BF16)
    gqk = jnp.concatenate([jnp.tile(q_norm[l], N_HEADS) * (HEAD_DIM ** -0.5),
                           jnp.tile(k_norm[l], N_KV_HEADS)]).reshape(1, QK_WIDTH).astype(F32)
    head_of_col = np.arange(NORM_GROUP) // HEAD_DIM
    gsum = jnp.asarray(head_of_col[:, None] == head_of_col[None, :], dtype=BF16)
    wgu1, wd1 = ffn1_w_gu[l].astype(BF16), ffn1_w_down[l].astype(BF16)
    wgu2, wd2 = ffn2_w_gu[l].astype(BF16), ffn2_w_down[l].astype(BF16)
    cw, cb = jnp.repeat(conv_w[l].astype(F32), SUBLANES, axis=0), row(conv_b[l])
    lng, lnb = row(conv_ln_g[l]), row(conv_ln_b[l])

    band = WINDOW + CHUNK
    bias_p = _bias_rows(rel_bias_table, np.arange(CHUNK), np.arange(band) - WINDOW)
    key = np.arange(band)[None, :]
    neg = lambda first_valid: jnp.where(key >= first_valid, bias_p, -jnp.inf)
    bias_prompt = jnp.stack([neg(WINDOW), neg(WINDOW - CHUNK), bias_p])
    qpos_s = past_len + np.arange(DS)
    kpos_s = np.concatenate([past_len - R + np.arange(R), qpos_s])
    bias_sample = _bias_rows(rel_bias_table, qpos_s, kpos_s)[None]
    sink_col = lambda n: jnp.repeat(sinks[l].astype(F32), n).reshape(N_HEADS * n, 1)

    def layer(x, n_streams, seq, chunk, tile, past):
        x1, q, k, v, u = _ffn1_proj(x, row(ffn1_norm[l]), wgu1, wd1, row(mix_norm[l]), win, gsum, gqk)
        if past is None:
            mix = _mixer(q, k, v, u, k, v, u, bias_prompt, sink_col(chunk), cw, cb, lng, lnb,
                         n_streams=n_streams, seq=seq, chunk=chunk, tile=tile, past_is_own_stream=True)
        else:
            mix = _mixer(q, k, v, u, *past, bias_sample, sink_col(chunk), cw, cb, lng, lnb,
                         n_streams=n_streams, seq=seq, chunk=chunk, tile=tile, past_is_own_stream=False)
        y = _out_ffn2(x1, mix, wout, row(ffn2_norm[l]), wgu2, wd2, row(final_norm[l]))
        return y, k, v, u

    yp, kp, vp, up = layer(x_prompt.reshape(B * S, D_MODEL), B, S, CHUNK, MIX_TILE, None)
    ck = cache_k[l].reshape(DB * R, KV_WIDTH).astype(F32)
    cv = cache_v[l].reshape(DB * R, KV_WIDTH).astype(F32)
    sc = jnp.pad(state_conv[l].astype(F32), ((0, 0), (CONV_PAD - (CONV_K - 1), 0), (0, 0)))
    ys, ks, vs, us = layer(x_sample.reshape(DB * DS, D_MODEL), DB, DS, DS, DS,
                           (ck, cv, sc.reshape(DB * CONV_PAD, CONV_WIDTH)))

    RP = min(WINDOW, S)
    heads = lambda a, n, t: a.reshape(n, t, N_KV_HEADS, HEAD_DIM)
    new_k_prompt = heads(kp, B, S)[:, S - RP:][None]
    new_v_prompt = heads(vp, B, S)[:, S - RP:][None]
    new_conv_prompt = up.reshape(B, S, CONV_WIDTH)[:, S - (CONV_K - 1):][None]
    new_k_sample = jnp.concatenate([cache_k[l].astype(F32), heads(ks, DB, DS)], axis=1)[:, DS:][None]
    new_v_sample = jnp.concatenate([cache_v[l].astype(F32), heads(vs, DB, DS)], axis=1)[:, DS:][None]
    new_conv_sample = jnp.concatenate([state_conv[l].astype(F32), us.reshape(DB, DS, CONV_WIDTH)],
                                      axis=1)[:, DS:][None]
    return (yp.reshape(B, S, D_MODEL), ys.reshape(DB, DS, D_MODEL), new_k_prompt, new_v_prompt,
            new_conv_prompt, new_k_sample, new_v_sample, new_conv_sample)
```

```python
import functools
import math

import numpy as np
import jax
import jax.numpy as jnp
from jax import lax
from jax.experimental import pallas as pl
from jax.experimental.pallas import tpu as pltpu

D_MODEL = 1024
CHUNK = 64
HEAD_DIM = 64
N_HEADS = 8
N_KV_HEADS = 2
GROUP = N_HEADS // N_KV_HEADS
ATT_WIDTH = N_HEADS * HEAD_DIM
KV_WIDTH = N_KV_HEADS * HEAD_DIM
CONV_WIDTH = D_MODEL - ATT_WIDTH
CONV_K = 31
WINDOW = 128
IN_WIDTH = ATT_WIDTH + 2 * KV_WIDTH + 2 * CONV_WIDTH
QK_WIDTH = ATT_WIDTH + KV_WIDTH
D_FF = 2816
NUM_BUCKETS = 32
MAX_DISTANCE = 128
EPS = 1e-6

LANES = 128
CONV_PAD = 32
FF_SPLITS = (0, 1536, D_FF)
SUBLANES = 8
NORM_GROUP = 256
CONV_ROWS = 32
TOKEN_TILE = 512
MIX_TILE = 512
VMEM_LIMIT = 56 * 1024 * 1024

F32 = jnp.float32
BF16 = jnp.bfloat16


def _dot(a, b):
    return jnp.dot(a, b, preferred_element_type=F32)


def _rms(x, g):
    ms = jnp.mean(x * x, axis=-1, keepdims=True)
    return x * lax.rsqrt(ms + EPS) * g


def _swiglu(xn, wgu_ref, wd_ref):
    acc = None
    for lo, hi in zip(FF_SPLITS[:-1], FF_SPLITS[1:]):
        gate = _dot(xn, wgu_ref[:, lo:hi])
        up = _dot(xn, wgu_ref[:, D_FF + lo:D_FF + hi])
        act = (gate * jax.nn.sigmoid(gate) * up).astype(BF16)
        part = _dot(act, wd_ref[lo:hi, :])
        acc = part if acc is None else acc + part
    return acc


def _ffn1_proj_kernel(x_ref, g1_ref, wgu_ref, wd_ref, gmix_ref, win_ref, gsum_ref, gqk_ref,
                      x1_ref, q_ref, k_ref, v_ref, u_ref):
    x = x_ref[...]
    x1 = x + 0.5 * _swiglu(_rms(x, g1_ref[...]).astype(BF16), wgu_ref, wd_ref)
    x1_ref[...] = x1
    z = _dot(_rms(x1, gmix_ref[...]).astype(BF16), win_ref[...])
    qk = z[:, :QK_WIDTH]
    sq = (qk * qk).astype(BF16)
    ss = jnp.concatenate(
        [_dot(sq[:, lo:lo + NORM_GROUP], gsum_ref[...]) for lo in range(0, ATT_WIDTH, NORM_GROUP)]
        + [_dot(sq[:, ATT_WIDTH:], gsum_ref[:KV_WIDTH, :KV_WIDTH])], axis=1)
    qkn = qk * lax.rsqrt(ss * (1.0 / HEAD_DIM) + EPS) * gqk_ref[...]
    q_ref[...] = qkn[:, :ATT_WIDTH].astype(BF16)
    k_ref[...] = qkn[:, ATT_WIDTH:]
    v_ref[...] = z[:, QK_WIDTH:QK_WIDTH + KV_WIDTH]
    a = z[:, QK_WIDTH + KV_WIDTH:QK_WIDTH + KV_WIDTH + CONV_WIDTH]
    g = z[:, QK_WIDTH + KV_WIDTH + CONV_WIDTH:]
    u_ref[...] = a * jax.nn.sigmoid(g)


def _out_ffn2_kernel(x1_ref, mix_ref, wout_ref, g2_ref, wgu_ref, wd_ref, gfin_ref, y_ref):
    x2 = x1_ref[...] + _dot(mix_ref[...], wout_ref[...])
    x3 = x2 + 0.5 * _swiglu(_rms(x2, g2_ref[...]).astype(BF16), wgu_ref, wd_ref)
    y_ref[...] = _rms(x3, gfin_ref[...])


def _mixer_kernel(*refs, chunk, tile):
    _mixer_tile(pl.program_id(1), *refs, chunk=chunk, tile=tile, first_tile_has_no_past=False)


def _mixer_ffn2_kernel(q_ref, kc_ref, kp_ref, vc_ref, vp_ref, uc_ref, up_ref, bias_ref, sink_ref,
                       cw_ref, cb_ref, lng_ref, lnb_ref,
                       x1_ref, wout_ref, g2_ref, wgu_ref, wd_ref, gfin_ref,
                       y_ref, mix_ref, kw_ref, vw_ref, ue_ref, us_ref, *, chunk, tile, tiles_per_stream):
    s = pl.program_id(0)

    @pl.when(s == 0)
    def _():
        mix_ref[...] = jnp.zeros_like(mix_ref)

    _out_ffn2_kernel(x1_ref, mix_ref, wout_ref, g2_ref, wgu_ref, wd_ref, gfin_ref, y_ref)
    t = jnp.minimum(s, pl.num_programs(0) - 2)
    _mixer_tile(lax.rem(t, tiles_per_stream), q_ref, kc_ref, kp_ref, vc_ref, vp_ref, uc_ref, up_ref,
                bias_ref, sink_ref, cw_ref, cb_ref, lng_ref, lnb_ref, mix_ref, kw_ref, vw_ref, ue_ref,
                us_ref, chunk=chunk, tile=tile, first_tile_has_no_past=True)


def _mixer_tile(i, q_ref, kc_ref, kp_ref, vc_ref, vp_ref, uc_ref, up_ref, bias_ref, sink_ref,
                cw_ref, cb_ref, lng_ref, lnb_ref, mix_ref, kw_ref, vw_ref, ue_ref, us_ref,
                *, chunk, tile, first_tile_has_no_past):
    kw_ref[0:WINDOW, :] = kp_ref[...].astype(BF16)
    kw_ref[WINDOW:, :] = kc_ref[...].astype(BF16)
    vw_ref[0:WINDOW, :] = vp_ref[...].astype(BF16)
    vw_ref[WINDOW:, :] = vc_ref[...].astype(BF16)
    u_prev = up_ref[...]
    if first_tile_has_no_past:
        u_prev = jnp.where(i > 0, u_prev, 0.0)
    ue_ref[0:CONV_PAD, :] = u_prev
    ue_ref[CONV_PAD:, :] = uc_ref[...]

    kwin_len = WINDOW + chunk
    lane = lax.broadcasted_iota(jnp.int32, (chunk, LANES), 1)
    low_half = lane < HEAD_DIM
    sink = sink_ref[...]
    n_bias = bias_ref.shape[0]
    for c in range(tile // chunk):
        qc = q_ref[c * chunk:(c + 1) * chunk, :]
        zero = jnp.zeros_like(qc[:, :LANES])
        blocks = []
        for j in range(N_KV_HEADS):
            keep = low_half if j == 0 else jnp.logical_not(low_half)
            for p in range(GROUP):
                blocks.append(jnp.where(keep, qc[:, p * LANES:(p + 1) * LANES], zero))
        q_big = jnp.concatenate(blocks, axis=0)
        kwin = kw_ref[c * chunk:c * chunk + kwin_len, :]
        vwin = vw_ref[c * chunk:c * chunk + kwin_len, :]
        s = lax.dot_general(kwin, q_big, (((1,), (1,)), ((), ())), preferred_element_type=F32)
        bidx = jnp.minimum(i * (tile // chunk) + c, n_bias - 1)
        s = s + bias_ref[bidx]
        m = jnp.maximum(jnp.max(s, axis=0, keepdims=True), sink)
        e = jnp.exp(s - m)
        denom = jnp.sum(e, axis=0, keepdims=True) + jnp.exp(sink - m)
        prob = (e * (1.0 / denom)).astype(BF16)
        o = lax.dot_general(prob, vwin, (((0,), (0,)), ((), ())), preferred_element_type=F32)
        for p in range(GROUP):
            o0 = o[p * chunk:(p + 1) * chunk, :]
            o1 = o[(GROUP + p) * chunk:(GROUP + p + 1) * chunk, :]
            mix_ref[c * chunk:(c + 1) * chunk, p * LANES:(p + 1) * LANES] = (
                jnp.where(low_half, o0, o1).astype(BF16))

    shifted_rows = us_ref.shape[1]
    for b in range(1, SUBLANES):
        us_ref[b - 1] = ue_ref[b:b + shifted_rows, :]
    rows = min(CONV_ROWS, tile)
    first_tap = CONV_PAD - (CONV_K - 1)
    for r in range(tile // rows):
        acc = jnp.zeros((rows, CONV_WIDTH), F32)
        for j in range(CONV_K):
            a, b = divmod(first_tap + j, SUBLANES)
            lo = r * rows + a * SUBLANES
            taps = ue_ref[lo:lo + rows, :] if b == 0 else us_ref[b - 1, lo:lo + rows, :]
            w = cw_ref[j * SUBLANES:(j + 1) * SUBLANES, :]
            acc = acc + taps * jnp.concatenate([w] * (rows // SUBLANES), axis=0)
        y = acc + cb_ref[...]
        mu = jnp.mean(y, axis=-1, keepdims=True)
        yc = y - mu
        var = jnp.mean(yc * yc, axis=-1, keepdims=True)
        yn = yc * lax.rsqrt(var + EPS) * lng_ref[...] + lnb_ref[...]
        mix_ref[r * rows:(r + 1) * rows, ATT_WIDTH:] = (yn * jax.nn.sigmoid(yn)).astype(BF16)


def _const_spec(shape):
    return pl.BlockSpec(shape, lambda *_: (0,) * len(shape), pipeline_mode=pl.Buffered(1))


def _params(n_axes):
    return pltpu.CompilerParams(dimension_semantics=("arbitrary",) * n_axes,
                                vmem_limit_bytes=VMEM_LIMIT)


def _ffn1_proj(x, g1, wgu, wd, gmix, win, gsum, gqk):
    n = x.shape[0]
    tm = min(TOKEN_TILE, n)
    row = lambda w: pl.BlockSpec((tm, w), lambda i: (i, 0))
    return pl.pallas_call(
        _ffn1_proj_kernel,
        grid=(n // tm,),
        in_specs=[row(D_MODEL), _const_spec(g1.shape), _const_spec(wgu.shape), _const_spec(wd.shape),
                  _const_spec(gmix.shape), _const_spec(win.shape), _const_spec(gsum.shape),
                  _const_spec(gqk.shape)],
        out_specs=[row(D_MODEL), row(ATT_WIDTH), row(KV_WIDTH), row(KV_WIDTH), row(CONV_WIDTH)],
        out_shape=[jax.ShapeDtypeStruct((n, D_MODEL), F32), jax.ShapeDtypeStruct((n, ATT_WIDTH), BF16),
                   jax.ShapeDtypeStruct((n, KV_WIDTH), F32), jax.ShapeDtypeStruct((n, KV_WIDTH), F32),
                   jax.ShapeDtypeStruct((n, CONV_WIDTH), F32)],
        compiler_params=_params(1),
        name="ffn1_proj",
    )(x, g1, wgu, wd, gmix, win, gsum, gqk)


def _out_ffn2(x1, mix, wout, g2, wgu, wd, gfin):
    n = x1.shape[0]
    tm = min(TOKEN_TILE, n)
    row = lambda w: pl.BlockSpec((tm, w), lambda i: (i, 0))
    return pl.pallas_call(
        _out_ffn2_kernel,
        grid=(n // tm,),
        in_specs=[row(D_MODEL), row(D_MODEL), _const_spec(wout.shape), _const_spec(g2.shape),
                  _const_spec(wgu.shape), _const_spec(wd.shape), _const_spec(gfin.shape)],
        out_specs=row(D_MODEL),
        out_shape=jax.ShapeDtypeStruct((n, D_MODEL), F32),
        compiler_params=_params(1),
        name="out_ffn2",
    )(x1, mix, wout, g2, wgu, wd, gfin)


def _mixer_scratch(tile):
    return [pltpu.VMEM((WINDOW + tile, KV_WIDTH), BF16),
            pltpu.VMEM((WINDOW + tile, KV_WIDTH), BF16),
            pltpu.VMEM((CONV_PAD + tile, CONV_WIDTH), F32),
            pltpu.VMEM((SUBLANES - 1, CONV_PAD + tile - SUBLANES, CONV_WIDTH), F32)]


def _mixer(q, k, v, u, k_past, v_past, u_past, bias, sink_col, cw, cb, lng, lnb,
           *, n_streams, seq, chunk, tile):
    nt = seq // tile
    cur = lambda w: pl.BlockSpec((tile, w), lambda b, i: (b * nt + i, 0))
    kv_past = pl.BlockSpec((WINDOW, KV_WIDTH), lambda b, i: (b, 0))
    u_past_spec = pl.BlockSpec((CONV_PAD, CONV_WIDTH), lambda b, i: (b, 0))
    body = functools.partial(_mixer_kernel, chunk=chunk, tile=tile)
    return pl.pallas_call(
        body,
        grid=(n_streams, nt),
        in_specs=[cur(ATT_WIDTH), cur(KV_WIDTH), kv_past, cur(KV_WIDTH), kv_past, cur(CONV_WIDTH),
                  u_past_spec, _const_spec(bias.shape), _const_spec(sink_col.shape),
                  _const_spec(cw.shape), _const_spec(cb.shape), _const_spec(lng.shape),
                  _const_spec(lnb.shape)],
        out_specs=cur(D_MODEL),
        out_shape=jax.ShapeDtypeStruct((n_streams * seq, D_MODEL), BF16),
        scratch_shapes=_mixer_scratch(tile),
        compiler_params=_params(2),
        name="mixer",
    )(q, k, k_past, v, v_past, u, u_past, bias, sink_col, cw, cb, lng, lnb)


def _mixer_ffn2(q, k, v, u, bias, sink_col, cw, cb, lng, lnb, x1, wout, g2, wgu, wd, gfin,
                *, seq, chunk, tile):
    n_tiles = q.shape[0] // tile
    mix_tile = lambda s: jnp.minimum(s, n_tiles - 1)
    ffn_tile = lambda s: jnp.maximum(s - 1, 0)
    cur = lambda w: pl.BlockSpec((tile, w), lambda s: (mix_tile(s), 0))
    kv_past = pl.BlockSpec(
        (WINDOW, KV_WIDTH), lambda s: (jnp.maximum(mix_tile(s) * (tile // WINDOW) - 1, 0), 0))
    u_past = pl.BlockSpec(
        (CONV_PAD, CONV_WIDTH), lambda s: (jnp.maximum(mix_tile(s) * (tile // CONV_PAD) - 1, 0), 0))
    ffn_row = pl.BlockSpec((tile, D_MODEL), lambda s: (ffn_tile(s), 0))
    body = functools.partial(_mixer_ffn2_kernel, chunk=chunk, tile=tile, tiles_per_stream=seq // tile)
    return pl.pallas_call(
        body,
        grid=(n_tiles + 1,),
        in_specs=[cur(ATT_WIDTH), cur(KV_WIDTH), kv_past, cur(KV_WIDTH), kv_past, cur(CONV_WIDTH),
                  u_past, _const_spec(bias.shape), _const_spec(sink_col.shape),
                  _const_spec(cw.shape), _const_spec(cb.shape), _const_spec(lng.shape),
                  _const_spec(lnb.shape),
                  ffn_row, _const_spec(wout.shape), _const_spec(g2.shape), _const_spec(wgu.shape),
                  _const_spec(wd.shape), _const_spec(gfin.shape)],
        out_specs=ffn_row,
        out_shape=jax.ShapeDtypeStruct((q.shape[0], D_MODEL), F32),
        scratch_shapes=[pltpu.VMEM((tile, D_MODEL), BF16)] + _mixer_scratch(tile),
        compiler_params=_params(1),
        name="mixer_ffn2",
    )(q, k, k, v, v, u, u, bias, sink_col, cw, cb, lng, lnb, x1, wout, g2, wgu, wd, gfin)


def _rel_bucket(rel):
    nb = NUM_BUCKETS // 2
    max_exact = nb // 2
    ret = np.where(rel > 0, nb, 0)
    n = np.abs(rel)
    nf = np.maximum(n, 1).astype(np.float32)
    large = max_exact + (np.log(nf / np.float32(max_exact)) / np.float32(math.log(MAX_DISTANCE / max_exact))
                         * np.float32(nb - max_exact)).astype(np.int32)
    large = np.minimum(large, nb - 1)
    return (ret + np.where(n < max_exact, n, large)).astype(np.int32)


def _bias_rows(table, qpos, kpos):
    bucket = _rel_bucket(kpos[:, None] - qpos[None, :])
    b = jnp.transpose(table[bucket], (0, 2, 1)).astype(F32)
    return b.reshape(kpos.shape[0], N_HEADS * qpos.shape[0])


def _head_pair_perm():
    cols = []
    for p in range(GROUP):
        for h in (p, GROUP + p):
            cols.extend(range(h * HEAD_DIM, (h + 1) * HEAD_DIM))
    return np.asarray(cols, dtype=np.int32)


def kernel(x_prompt, x_sample, cache_k, cache_v, state_conv, rel_bias_table, ffn1_norm, ffn1_w_gu,
           ffn1_w_down, mix_norm, w_in, q_norm, k_norm, sinks, conv_w, conv_b, conv_ln_g, conv_ln_b,
           w_out, ffn2_norm, ffn2_w_gu, ffn2_w_down, final_norm):
    B, S, _ = x_prompt.shape
    DB, DS, _ = x_sample.shape
    R = cache_k.shape[2]
    past_len = R
    assert ffn1_norm.shape[0] == 1 and R == WINDOW and S % MIX_TILE == 0 and DS % 8 == 0
    l = 0
    row = lambda a: a.reshape(1, -1).astype(F32)

    perm = _head_pair_perm()
    win = jnp.concatenate([w_in[l][:, perm], w_in[l][:, ATT_WIDTH:]], axis=1).astype(BF16)
    wout = jnp.concatenate([w_out[l][perm, :], w_out[l][ATT_WIDTH:, :]], axis=0).astype(BF16)
    gqk = jnp.concatenate([jnp.tile(q_norm[l], N_HEADS) * (HEAD_DIM ** -0.5),
                           jnp.tile(k_norm[l], N_KV_HEADS)]).reshape(1, QK_WIDTH).astype(F32)
    head_of_col = np.arange(NORM_GROUP) // HEAD_DIM
    gsum = jnp.asarray(head_of_col[:, None] == head_of_col[None, :], dtype=BF16)
    wgu1, wd1 = ffn1_w_gu[l].astype(BF16), ffn1_w_down[l].astype(BF16)
    wgu2, wd2 = ffn2_w_gu[l].astype(BF16), ffn2_w_down[l].astype(BF16)
    cw, cb = jnp.repeat(conv_w[l].astype(F32), SUBLANES, axis=0), row(conv_b[l])
    lng, lnb = row(conv_ln_g[l]), row(conv_ln_b[l])

    band = WINDOW + CHUNK
    bias_p = _bias_rows(rel_bias_table, np.arange(CHUNK), np.arange(band) - WINDOW)
    key = np.arange(band)[:, None]
    neg = lambda first_valid: jnp.where(key >= first_valid, bias_p, -jnp.inf)
    bias_prompt = jnp.stack([neg(WINDOW), neg(WINDOW - CHUNK), bias_p])
    qpos_s = past_len + np.arange(DS)
    kpos_s = np.concatenate([past_len - R + np.arange(R), qpos_s])
    bias_sample = _bias_rows(rel_bias_table, qpos_s, kpos_s)[None]
    sink_col = lambda n: jnp.repeat(sinks[l].astype(F32), n).reshape(1, N_HEADS * n)

    def layer(x, n_streams, seq, chunk, tile, past):
        x1, q, k, v, u = _ffn1_proj(x, row(ffn1_norm[l]), wgu1, wd1, row(mix_norm[l]), win, gsum, gqk)
        if past is None:
            y = _mixer_ffn2(q, k, v, u, bias_prompt, sink_col(chunk), cw, cb, lng, lnb, x1, wout,
                            row(ffn2_norm[l]), wgu2, wd2, row(final_norm[l]),
                            seq=seq, chunk=chunk, tile=tile)
        else:
            mix = _mixer(q, k, v, u, *past, bias_sample, sink_col(chunk), cw, cb, lng, lnb,
                         n_streams=n_streams, seq=seq, chunk=chunk, tile=tile)
            y = _out_ffn2(x1, mix, wout, row(ffn2_norm[l]), wgu2, wd2, row(final_norm[l]))
        return y, k, v, u

    yp, kp, vp, up = layer(x_prompt.reshape(B * S, D_MODEL), B, S, CHUNK, MIX_TILE, None)
    ck = cache_k[l].reshape(DB * R, KV_WIDTH).astype(F32)
    cv = cache_v[l].reshape(DB * R, KV_WIDTH).astype(F32)
    sc = jnp.pad(state_conv[l].astype(F32), ((0, 0), (CONV_PAD - (CONV_K - 1), 0), (0, 0)))
    ys, ks, vs, us = layer(x_sample.reshape(DB * DS, D_MODEL), DB, DS, DS, DS,
                           (ck, cv, sc.reshape(DB * CONV_PAD, CONV_WIDTH)))

    RP = min(WINDOW, S)
    heads = lambda a, n, t: a.reshape(n, t, N_KV_HEADS, HEAD_DIM)
    new_k_prompt = heads(kp, B, S)[:, S - RP:][None]
    new_v_prompt = heads(vp, B, S)[:, S - RP:][None]
    new_conv_prompt = up.reshape(B, S, CONV_WIDTH)[:, S - (CONV_K - 1):][None]
    new_k_sample = jnp.concatenate([cache_k[l].astype(F32), heads(ks, DB, DS)], axis=1)[:, DS:][None]
    new_v_sample = jnp.concatenate([cache_v[l].astype(F32), heads(vs, DB, DS)], axis=1)[:, DS:][None]
    new_conv_sample = jnp.concatenate([state_conv[l].astype(F32), us.reshape(DB, DS, CONV_WIDTH)],
                                      axis=1)[:, DS:][None]
    return (yp.reshape(B, S, D_MODEL), ys.reshape(DB, DS, D_MODEL), new_k_prompt, new_v_prompt,
            new_conv_prompt, new_k_sample, new_v_sample, new_conv_sample)
```

```python
import functools
import math

import numpy as np
import jax
import jax.numpy as jnp
from jax import lax
from jax.experimental import pallas as pl
from jax.experimental.pallas import tpu as pltpu

D_MODEL = 1024
CHUNK = 64
HEAD_DIM = 64
N_HEADS = 8
N_KV_HEADS = 2
GROUP = N_HEADS // N_KV_HEADS
ATT_WIDTH = N_HEADS * HEAD_DIM
KV_WIDTH = N_KV_HEADS * HEAD_DIM
CONV_WIDTH = D_MODEL - ATT_WIDTH
CONV_K = 31
WINDOW = 128
IN_WIDTH = ATT_WIDTH + 2 * KV_WIDTH + 2 * CONV_WIDTH
QK_WIDTH = ATT_WIDTH + KV_WIDTH
D_FF = 2816
NUM_BUCKETS = 32
MAX_DISTANCE = 128
EPS = 1e-6

LANES = 128
CONV_PAD = 32
FF_SPLITS = (0, 1536, D_FF)
SUBLANES = 8
MXU_TILE = 256
NORM_GROUP = MXU_TILE
CONV_ROWS = 32
TOKEN_TILE = 512
MIX_TILE = 512
VMEM_LIMIT = 56 * 1024 * 1024

F32 = jnp.float32
BF16 = jnp.bfloat16


def _dot(a, b):
    return jnp.dot(a, b, preferred_element_type=F32)


def _rms(x, g):
    ms = jnp.mean(x * x, axis=-1, keepdims=True)
    return x * lax.rsqrt(ms + EPS) * g


def _swiglu(xn, wgu_ref, wd_ref):
    acc = None
    for lo, hi in zip(FF_SPLITS[:-1], FF_SPLITS[1:]):
        gate = _dot(xn, wgu_ref[:, lo:hi])
        up = _dot(xn, wgu_ref[:, D_FF + lo:D_FF + hi])
        act = (gate * jax.nn.sigmoid(gate) * up).astype(BF16)
        part = _dot(act, wd_ref[lo:hi, :])
        acc = part if acc is None else acc + part
    return acc


def _ffn1_proj_kernel(x_ref, g1_ref, wgu_ref, wd_ref, gmix_ref, win_ref, gsum_ref, gqk_ref,
                      x1_ref, q_ref, k_ref, v_ref, u_ref):
    x = x_ref[...]
    x1 = x + 0.5 * _swiglu(_rms(x, g1_ref[...]).astype(BF16), wgu_ref, wd_ref)
    x1_ref[...] = x1
    z = _dot(_rms(x1, gmix_ref[...]).astype(BF16), win_ref[...])
    qk = z[:, :QK_WIDTH]
    sq = (qk * qk).astype(BF16)
    ss = jnp.concatenate(
        [_dot(sq[:, lo:lo + NORM_GROUP], gsum_ref[...]) for lo in range(0, ATT_WIDTH, NORM_GROUP)]
        + [_dot(sq[:, ATT_WIDTH:], gsum_ref[:KV_WIDTH, :KV_WIDTH])], axis=1)
    qkn = qk * lax.rsqrt(ss * (1.0 / HEAD_DIM) + EPS) * gqk_ref[...]
    q_ref[...] = qkn[:, :ATT_WIDTH].astype(BF16)
    k_ref[...] = qkn[:, ATT_WIDTH:]
    v_ref[...] = z[:, QK_WIDTH:QK_WIDTH + KV_WIDTH]
    a = z[:, QK_WIDTH + KV_WIDTH:QK_WIDTH + KV_WIDTH + CONV_WIDTH]
    g = z[:, QK_WIDTH + KV_WIDTH + CONV_WIDTH:]
    u_ref[...] = a * jax.nn.sigmoid(g)


def _out_ffn2_kernel(x1_ref, mix_ref, wout_ref, g2_ref, wgu_ref, wd_ref, gfin_ref, y_ref):
    x2 = x1_ref[...] + _dot(mix_ref[...], wout_ref[...])
    x3 = x2 + 0.5 * _swiglu(_rms(x2, g2_ref[...]).astype(BF16), wgu_ref, wd_ref)
    y_ref[...] = _rms(x3, gfin_ref[...])


def _mixer_kernel(*refs, chunk, tile):
    stage_kv, attend, conv = _mixer_parts(pl.program_id(1), *refs, chunk=chunk, tile=tile,
                                          first_tile_has_no_past=False)
    stage_kv()
    for c in range(tile // chunk):
        attend(c)
    conv()


def _mixer_ffn2_kernel(q_ref, kc_ref, kp_ref, vc_ref, vp_ref, uc_ref, up_ref, bias_ref, sink_ref,
                       cw_ref, cb_ref, lng_ref, lnb_ref,
                       x1_ref, wout_ref, g2_ref, wgu_ref, wd_ref, gfin_ref,
                       y_ref, mix_ref, x2_ref, xn_ref, act_ref, kw_ref, vw_ref, ue_ref, us_ref,
                       *, chunk, tile, tiles_per_stream):
    s = pl.program_id(0)

    @pl.when(s == 0)
    def _():
        mix_ref[...] = jnp.zeros_like(mix_ref)

    t = jnp.minimum(s, pl.num_programs(0) - 2)
    stage_kv, attend, conv = _mixer_parts(
        lax.rem(t, tiles_per_stream), q_ref, kc_ref, kp_ref, vc_ref, vp_ref, uc_ref, up_ref, bias_ref,
        sink_ref, cw_ref, cb_ref, lng_ref, lnb_ref, mix_ref, kw_ref, vw_ref, ue_ref, us_ref,
        chunk=chunk, tile=tile, first_tile_has_no_past=True)
    cols = lambda n: slice(n * MXU_TILE, (n + 1) * MXU_TILE)

    stage_kv()
    for n in range(D_MODEL // MXU_TILE):
        x2_ref[:, cols(n)] = x1_ref[:, cols(n)] + _dot(mix_ref[...], wout_ref[:, cols(n)])
    xn_ref[...] = _rms(x2_ref[...], g2_ref[...]).astype(BF16)
    n_units, n_chunks = D_FF // MXU_TILE, tile // chunk
    for n in range(n_units):
        lo = n * MXU_TILE
        gate = _dot(xn_ref[...], wgu_ref[:, lo:lo + MXU_TILE])
        up = _dot(xn_ref[...], wgu_ref[:, D_FF + lo:D_FF + lo + MXU_TILE])
        act_ref[:, cols(n)] = (gate * jax.nn.sigmoid(gate) * up).astype(BF16)
        if n < n_chunks:
            attend(n)
    for c in range(n_units, n_chunks):
        attend(c)
    for n in range(D_MODEL // MXU_TILE):
        x2_ref[:, cols(n)] = x2_ref[:, cols(n)] + 0.5 * _dot(act_ref[...], wd_ref[:, cols(n)])
    y_ref[...] = _rms(x2_ref[...], gfin_ref[...])
    conv()


def _mixer_parts(i, q_ref, kc_ref, kp_ref, vc_ref, vp_ref, uc_ref, up_ref, bias_ref, sink_ref,
                 cw_ref, cb_ref, lng_ref, lnb_ref, mix_ref, kw_ref, vw_ref, ue_ref, us_ref,
                 *, chunk, tile, first_tile_has_no_past):
    def stage_kv():
        kw_ref[0:WINDOW, :] = kp_ref[...].astype(BF16)
        kw_ref[WINDOW:, :] = kc_ref[...].astype(BF16)
        vw_ref[0:WINDOW, :] = vp_ref[...].astype(BF16)
        vw_ref[WINDOW:, :] = vc_ref[...].astype(BF16)

    kwin_len = WINDOW + chunk
    n_bias = bias_ref.shape[0]

    def attend(c):
        low_half = lax.broadcasted_iota(jnp.int32, (chunk, LANES), 1) < HEAD_DIM
        sink = sink_ref[...]
        qc = q_ref[c * chunk:(c + 1) * chunk, :]
        zero = jnp.zeros_like(qc[:, :LANES])
        blocks = []
        for j in range(N_KV_HEADS):
            keep = low_half if j == 0 else jnp.logical_not(low_half)
            for p in range(GROUP):
                blocks.append(jnp.where(keep, qc[:, p * LANES:(p + 1) * LANES], zero))
        q_big = jnp.concatenate(blocks, axis=0)
        kwin = kw_ref[c * chunk:c * chunk + kwin_len, :]
        vwin = vw_ref[c * chunk:c * chunk + kwin_len, :]
        s = lax.dot_general(kwin, q_big, (((1,), (1,)), ((), ())), preferred_element_type=F32)
        bidx = jnp.minimum(i * (tile // chunk) + c, n_bias - 1)
        s = s + bias_ref[bidx]
        m = jnp.maximum(jnp.max(s, axis=0, keepdims=True), sink)
        e = jnp.exp(s - m)
        denom = jnp.sum(e, axis=0, keepdims=True) + jnp.exp(sink - m)
        prob = (e * (1.0 / denom)).astype(BF16)
        o = lax.dot_general(prob, vwin, (((0,), (0,)), ((), ())), preferred_element_type=F32)
        for p in range(GROUP):
            o0 = o[p * chunk:(p + 1) * chunk, :]
            o1 = o[(GROUP + p) * chunk:(GROUP + p + 1) * chunk, :]
            mix_ref[c * chunk:(c + 1) * chunk, p * LANES:(p + 1) * LANES] = (
                jnp.where(low_half, o0, o1).astype(BF16))

    def conv():
        u_prev = up_ref[...]
        if first_tile_has_no_past:
            u_prev = jnp.where(i > 0, u_prev, 0.0)
        ue_ref[0:CONV_PAD, :] = u_prev
        ue_ref[CONV_PAD:, :] = uc_ref[...]
        shifted_rows = us_ref.shape[1]
        for b in range(1, SUBLANES):
            us_ref[b - 1] = ue_ref[b:b + shifted_rows, :]
        rows = min(CONV_ROWS, tile)
        first_tap = CONV_PAD - (CONV_K - 1)
        for r in range(tile // rows):
            acc = jnp.zeros((rows, CONV_WIDTH), F32)
            for j in range(CONV_K):
                a, b = divmod(first_tap + j, SUBLANES)
                lo = r * rows + a * SUBLANES
                taps = ue_ref[lo:lo + rows, :] if b == 0 else us_ref[b - 1, lo:lo + rows, :]
                w = cw_ref[j * SUBLANES:(j + 1) * SUBLANES, :]
                acc = acc + taps * jnp.concatenate([w] * (rows // SUBLANES), axis=0)
            y = acc + cb_ref[...]
            mu = jnp.mean(y, axis=-1, keepdims=True)
            yc = y - mu
            var = jnp.mean(yc * yc, axis=-1, keepdims=True)
            yn = yc * lax.rsqrt(var + EPS) * lng_ref[...] + lnb_ref[...]
            mix_ref[r * rows:(r + 1) * rows, ATT_WIDTH:] = (yn * jax.nn.sigmoid(yn)).astype(BF16)

    return stage_kv, attend, conv


def _const_spec(shape):
    return pl.BlockSpec(shape, lambda *_: (0,) * len(shape), pipeline_mode=pl.Buffered(1))


def _params(n_axes):
    return pltpu.CompilerParams(dimension_semantics=("arbitrary",) * n_axes,
                                vmem_limit_bytes=VMEM_LIMIT)


def _ffn1_proj(x, g1, wgu, wd, gmix, win, gsum, gqk):
    n = x.shape[0]
    tm = min(TOKEN_TILE, n)
    row = lambda w: pl.BlockSpec((tm, w), lambda i: (i, 0))
    return pl.pallas_call(
        _ffn1_proj_kernel,
        grid=(n // tm,),
        in_specs=[row(D_MODEL), _const_spec(g1.shape), _const_spec(wgu.shape), _const_spec(wd.shape),
                  _const_spec(gmix.shape), _const_spec(win.shape), _const_spec(gsum.shape),
                  _const_spec(gqk.shape)],
        out_specs=[row(D_MODEL), row(ATT_WIDTH), row(KV_WIDTH), row(KV_WIDTH), row(CONV_WIDTH)],
        out_shape=[jax.ShapeDtypeStruct((n, D_MODEL), F32), jax.ShapeDtypeStruct((n, ATT_WIDTH), BF16),
                   jax.ShapeDtypeStruct((n, KV_WIDTH), F32), jax.ShapeDtypeStruct((n, KV_WIDTH), F32),
                   jax.ShapeDtypeStruct((n, CONV_WIDTH), F32)],
        compiler_params=_params(1),
        name="ffn1_proj",
    )(x, g1, wgu, wd, gmix, win, gsum, gqk)


def _out_ffn2(x1, mix, wout, g2, wgu, wd, gfin):
    n = x1.shape[0]
    tm = min(TOKEN_TILE, n)
    row = lambda w: pl.BlockSpec((tm, w), lambda i: (i, 0))
    return pl.pallas_call(
        _out_ffn2_kernel,
        grid=(n // tm,),
        in_specs=[row(D_MODEL), row(D_MODEL), _const_spec(wout.shape), _const_spec(g2.shape),
                  _const_spec(wgu.shape), _const_spec(wd.shape), _const_spec(gfin.shape)],
        out_specs=row(D_MODEL),
        out_shape=jax.ShapeDtypeStruct((n, D_MODEL), F32),
        compiler_params=_params(1),
        name="out_ffn2",
    )(x1, mix, wout, g2, wgu, wd, gfin)


def _mixer_scratch(tile):
    return [pltpu.VMEM((WINDOW + tile, KV_WIDTH), BF16),
            pltpu.VMEM((WINDOW + tile, KV_WIDTH), BF16),
            pltpu.VMEM((CONV_PAD + tile, CONV_WIDTH), F32),
            pltpu.VMEM((SUBLANES - 1, CONV_PAD + tile - SUBLANES, CONV_WIDTH), F32)]


def _mixer(q, k, v, u, k_past, v_past, u_past, bias, sink_row, cw, cb, lng, lnb,
           *, n_streams, seq, chunk, tile):
    nt = seq // tile
    cur = lambda w: pl.BlockSpec((tile, w), lambda b, i: (b * nt + i, 0))
    kv_past = pl.BlockSpec((WINDOW, KV_WIDTH), lambda b, i: (b, 0))
    u_past_spec = pl.BlockSpec((CONV_PAD, CONV_WIDTH), lambda b, i: (b, 0))
    body = functools.partial(_mixer_kernel, chunk=chunk, tile=tile)
    return pl.pallas_call(
        body,
        grid=(n_streams, nt),
        in_specs=[cur(ATT_WIDTH), cur(KV_WIDTH), kv_past, cur(KV_WIDTH), kv_past, cur(CONV_WIDTH),
                  u_past_spec, _const_spec(bias.shape), _const_spec(sink_row.shape),
                  _const_spec(cw.shape), _const_spec(cb.shape), _const_spec(lng.shape),
                  _const_spec(lnb.shape)],
        out_specs=cur(D_MODEL),
        out_shape=jax.ShapeDtypeStruct((n_streams * seq, D_MODEL), BF16),
        scratch_shapes=_mixer_scratch(tile),
        compiler_params=_params(2),
        name="mixer",
    )(q, k, k_past, v, v_past, u, u_past, bias, sink_row, cw, cb, lng, lnb)


def _mixer_ffn2(q, k, v, u, bias, sink_row, cw, cb, lng, lnb, x1, wout, g2, wgu, wd, gfin,
                *, seq, chunk, tile):
    n_tiles = q.shape[0] // tile
    mix_tile = lambda s: jnp.minimum(s, n_tiles - 1)
    ffn_tile = lambda s: jnp.maximum(s - 1, 0)
    cur = lambda w: pl.BlockSpec((tile, w), lambda s: (mix_tile(s), 0))
    kv_past = pl.BlockSpec(
        (WINDOW, KV_WIDTH), lambda s: (jnp.maximum(mix_tile(s) * (tile // WINDOW) - 1, 0), 0))
    u_past = pl.BlockSpec(
        (CONV_PAD, CONV_WIDTH), lambda s: (jnp.maximum(mix_tile(s) * (tile // CONV_PAD) - 1, 0), 0))
    ffn_row = pl.BlockSpec((tile, D_MODEL), lambda s: (ffn_tile(s), 0))
    body = functools.partial(_mixer_ffn2_kernel, chunk=chunk, tile=tile, tiles_per_stream=seq // tile)
    return pl.pallas_call(
        body,
        grid=(n_tiles + 1,),
        in_specs=[cur(ATT_WIDTH), cur(KV_WIDTH), kv_past, cur(KV_WIDTH), kv_past, cur(CONV_WIDTH),
                  u_past, _const_spec(bias.shape), _const_spec(sink_row.shape),
                  _const_spec(cw.shape), _const_spec(cb.shape), _const_spec(lng.shape),
                  _const_spec(lnb.shape),
                  ffn_row, _const_spec(wout.shape), _const_spec(g2.shape), _const_spec(wgu.shape),
                  _const_spec(wd.shape), _const_spec(gfin.shape)],
        out_specs=ffn_row,
        out_shape=jax.ShapeDtypeStruct((q.shape[0], D_MODEL), F32),
        scratch_shapes=[pltpu.VMEM((tile, D_MODEL), BF16), pltpu.VMEM((tile, D_MODEL), F32),
                        pltpu.VMEM((tile, D_MODEL), BF16), pltpu.VMEM((tile, D_FF), BF16)]
        + _mixer_scratch(tile),
        compiler_params=_params(1),
        name="mixer_ffn2",
    )(q, k, k, v, v, u, u, bias, sink_row, cw, cb, lng, lnb, x1, wout, g2, wgu, wd, gfin)


def _rel_bucket(rel):
    nb = NUM_BUCKETS // 2
    max_exact = nb // 2
    ret = np.where(rel > 0, nb, 0)
    n = np.abs(rel)
    nf = np.maximum(n, 1).astype(np.float32)
    large = max_exact + (np.log(nf / np.float32(max_exact)) / np.float32(math.log(MAX_DISTANCE / max_exact))
                         * np.float32(nb - max_exact)).astype(np.int32)
    large = np.minimum(large, nb - 1)
    return (ret + np.where(n < max_exact, n, large)).astype(np.int32)


def _bias_cols(table, qpos, kpos):
    bucket = _rel_bucket(kpos[:, None] - qpos[None, :])
    b = jnp.transpose(table[bucket], (0, 2, 1)).astype(F32)
    return b.reshape(kpos.shape[0], N_HEADS * qpos.shape[0])


def kernel(x_prompt, x_sample, cache_k, cache_v, state_conv, rel_bias_table, ffn1_norm, ffn1_w_gu,
           ffn1_w_down, mix_norm, w_in, q_norm, k_norm, sinks, conv_w, conv_b, conv_ln_g, conv_ln_b,
           w_out, ffn2_norm, ffn2_w_gu, ffn2_w_down, final_norm):
    B, S, _ = x_prompt.shape
    DB, DS, _ = x_sample.shape
    R = cache_k.shape[2]
    past_len = R
    assert ffn1_norm.shape[0] == 1 and R == WINDOW and S % MIX_TILE == 0 and DS % 8 == 0
    l = 0
    row = lambda a: a.reshape(1, -1).astype(F32)

    pair = lambda a, axis: jnp.swapaxes(
        a.reshape(a.shape[:axis] + (N_KV_HEADS, GROUP, HEAD_DIM) + a.shape[axis + 1:]),
        axis, axis + 1).reshape(a.shape)
    win = jnp.concatenate([pair(w_in[l][:, :ATT_WIDTH], 1), w_in[l][:, ATT_WIDTH:]], axis=1).astype(BF16)
    wout = jnp.concatenate([pair(w_out[l][:ATT_WIDTH], 0), w_out[l][ATT_WIDTH:]], axis=0).astype(BF16)
    gqk = jnp.concatenate([jnp.tile(q_norm[l], N_HEADS) * (HEAD_DIM ** -0.5),
                           jnp.tile(k_norm[l], N_KV_HEADS)]).reshape(1, QK_WIDTH).astype(F32)
    head_of_col = np.arange(NORM_GROUP) // HEAD_DIM
    gsum = jnp.asarray(head_of_col[:, None] == head_of_col[None, :], dtype=BF16)
    wgu1, wd1 = ffn1_w_gu[l].astype(BF16), ffn1_w_down[l].astype(BF16)
    wgu2, wd2 = ffn2_w_gu[l].astype(BF16), ffn2_w_down[l].astype(BF16)
    cw, cb = jnp.repeat(conv_w[l].astype(F32), SUBLANES, axis=0), row(conv_b[l])
    lng, lnb = row(conv_ln_g[l]), row(conv_ln_b[l])

    band = WINDOW + CHUNK
    bias_p = _bias_cols(rel_bias_table, np.arange(CHUNK), np.arange(band) - WINDOW)
    key = np.arange(band)[:, None]
    neg = lambda first_valid: jnp.where(key >= first_valid, bias_p, -jnp.inf)
    bias_prompt = jnp.stack([neg(WINDOW), neg(WINDOW - CHUNK), bias_p])
    qpos_s = past_len + np.arange(DS)
    kpos_s = np.concatenate([past_len - R + np.arange(R), qpos_s])
    bias_sample = _bias_cols(rel_bias_table, qpos_s, kpos_s)[None]
    sink_row = lambda n: jnp.repeat(sinks[l].astype(F32), n).reshape(1, N_HEADS * n)

    def layer(x, n_streams, seq, chunk, tile, past):
        x1, q, k, v, u = _ffn1_proj(x, row(ffn1_norm[l]), wgu1, wd1, row(mix_norm[l]), win, gsum, gqk)
        if past is None:
            y = _mixer_ffn2(q, k, v, u, bias_prompt, sink_row(chunk), cw, cb, lng, lnb, x1, wout,
                            row(ffn2_norm[l]), wgu2, wd2, row(final_norm[l]),
                            seq=seq, chunk=chunk, tile=tile)
        else:
            mix = _mixer(q, k, v, u, *past, bias_sample, sink_row(chunk), cw, cb, lng, lnb,
                         n_streams=n_streams, seq=seq, chunk=chunk, tile=tile)
            y = _out_ffn2(x1, mix, wout, row(ffn2_norm[l]), wgu2, wd2, row(final_norm[l]))
        return y, k, v, u

    yp, kp, vp, up = layer(x_prompt.reshape(B * S, D_MODEL), B, S, CHUNK, MIX_TILE, None)
    ck = cache_k[l].reshape(DB * R, KV_WIDTH).astype(F32)
    cv = cache_v[l].reshape(DB * R, KV_WIDTH).astype(F32)
    sc = jnp.pad(state_conv[l].astype(F32), ((0, 0), (CONV_PAD - (CONV_K - 1), 0), (0, 0)))
    ys, ks, vs, us = layer(x_sample.reshape(DB * DS, D_MODEL), DB, DS, DS, DS,
                           (ck, cv, sc.reshape(DB * CONV_PAD, CONV_WIDTH)))

    RP = min(WINDOW, S)
    heads = lambda a, n, t: a.reshape(n, t, N_KV_HEADS, HEAD_DIM)
    tail = lambda a, n: a.reshape(B, S, a.shape[-1])[:, S - n:]
    new_k_prompt = heads(tail(kp, RP), B, RP)[None]
    new_v_prompt = heads(tail(vp, RP), B, RP)[None]
    new_conv_prompt = tail(up, CONV_K - 1)[None]
    new_k_sample = jnp.concatenate([cache_k[l].astype(F32), heads(ks, DB, DS)], axis=1)[:, DS:][None]
    new_v_sample = jnp.concatenate([cache_v[l].astype(F32), heads(vs, DB, DS)], axis=1)[:, DS:][None]
    new_conv_sample = jnp.concatenate([state_conv[l].astype(F32), us.reshape(DB, DS, CONV_WIDTH)],
                                      axis=1)[:, DS:][None]
    return (yp.reshape(B, S, D_MODEL), ys.reshape(DB, DS, D_MODEL), new_k_prompt, new_v_prompt,
            new_conv_prompt, new_k_sample, new_v_sample, new_conv_sample)
```

```python
import functools
import math

import numpy as np
import jax
import jax.numpy as jnp
from jax import lax
from jax.experimental import pallas as pl
from jax.experimental.pallas import tpu as pltpu

D_MODEL = 1024
CHUNK = 64
HEAD_DIM = 64
N_HEADS = 8
N_KV_HEADS = 2
GROUP = N_HEADS // N_KV_HEADS
ATT_WIDTH = N_HEADS * HEAD_DIM
KV_WIDTH = N_KV_HEADS * HEAD_DIM
CONV_WIDTH = D_MODEL - ATT_WIDTH
CONV_K = 31
WINDOW = 128
IN_WIDTH = ATT_WIDTH + 2 * KV_WIDTH + 2 * CONV_WIDTH
QK_WIDTH = ATT_WIDTH + KV_WIDTH
D_FF = 2816
NUM_BUCKETS = 32
MAX_DISTANCE = 128
EPS = 1e-6

LANES = 128
CONV_PAD = 32
FF_SPLITS = (0, 1536, D_FF)
SUBLANES = 8
MXU_TILE = 256
NORM_GROUP = MXU_TILE
CONV_ROWS = 32
TOKEN_TILE = 1024
MIX_TILE = 512
VMEM_LIMIT = 56 * 1024 * 1024

F32 = jnp.float32
BF16 = jnp.bfloat16


def _dot(a, b):
    return jnp.dot(a, b, preferred_element_type=F32)


def _rms(x, g):
    ms = jnp.mean(x * x, axis=-1, keepdims=True)
    return x * lax.rsqrt(ms + EPS) * g


def _swiglu(xn, wgu_ref, wd_ref):
    acc = None
    for lo, hi in zip(FF_SPLITS[:-1], FF_SPLITS[1:]):
        gate = _dot(xn, wgu_ref[:, lo:hi])
        up = _dot(xn, wgu_ref[:, D_FF + lo:D_FF + hi])
        act = (gate * jax.nn.sigmoid(gate) * up).astype(BF16)
        part = _dot(act, wd_ref[lo:hi, :])
        acc = part if acc is None else acc + part
    return acc


def _ffn1_proj_kernel(x_ref, g1_ref, wgu_ref, wd_ref, gmix_ref, win_ref, gsum_ref, gqk_ref,
                      x1_ref, q_ref, k_ref, v_ref, u_ref):
    x = x_ref[...]
    x1 = x + 0.5 * _swiglu(_rms(x, g1_ref[...]).astype(BF16), wgu_ref, wd_ref)
    x1_ref[...] = x1
    z = _dot(_rms(x1, gmix_ref[...]).astype(BF16), win_ref[...])
    qk = z[:, :QK_WIDTH]
    sq = (qk * qk).astype(BF16)
    ss = jnp.concatenate(
        [_dot(sq[:, lo:lo + NORM_GROUP], gsum_ref[...]) for lo in range(0, ATT_WIDTH, NORM_GROUP)]
        + [_dot(sq[:, ATT_WIDTH:], gsum_ref[:KV_WIDTH, :KV_WIDTH])], axis=1)
    qkn = qk * lax.rsqrt(ss * (1.0 / HEAD_DIM) + EPS) * gqk_ref[...]
    q_ref[...] = qkn[:, :ATT_WIDTH].astype(BF16)
    k_ref[...] = qkn[:, ATT_WIDTH:]
    v_ref[...] = z[:, QK_WIDTH:QK_WIDTH + KV_WIDTH]
    a = z[:, QK_WIDTH + KV_WIDTH:QK_WIDTH + KV_WIDTH + CONV_WIDTH]
    g = z[:, QK_WIDTH + KV_WIDTH + CONV_WIDTH:]
    u_ref[...] = a * jax.nn.sigmoid(g)


def _out_ffn2_kernel(x1_ref, mix_ref, wout_ref, g2_ref, wgu_ref, wd_ref, gfin_ref, y_ref):
    x2 = x1_ref[...] + _dot(mix_ref[...], wout_ref[...])
    x3 = x2 + 0.5 * _swiglu(_rms(x2, g2_ref[...]).astype(BF16), wgu_ref, wd_ref)
    y_ref[...] = _rms(x3, gfin_ref[...])


def _mixer_kernel(*refs, chunk, tile):
    stage, attend, conv = _mixer_parts(pl.program_id(1), *refs, chunk=chunk, tile=tile,
                                       first_tile_has_no_past=False)
    stage()
    for c in range(tile // chunk):
        attend(c)
    shift, conv_block, n_blocks = conv
    shift()
    for r in range(n_blocks):
        conv_block(r)


def _mixer_ffn2_kernel(q_ref, kc_ref, kp_ref, vc_ref, vp_ref, uc_ref, up_ref, bias_ref, sink_ref,
                       cw_ref, cb_ref, lng_ref, lnb_ref,
                       x1_ref, wout_ref, g2_ref, wgu_ref, wd_ref, gfin_ref,
                       y_ref, mix_ref, x2_ref, xn_ref, act_ref, kw_ref, vw_ref, ue_ref, us_ref,
                       *, chunk, tile, tiles_per_stream):
    s = pl.program_id(0)

    @pl.when(s == 0)
    def _():
        mix_ref[...] = jnp.zeros_like(mix_ref)

    t = jnp.minimum(s, pl.num_programs(0) - 2)
    stage, attend, conv = _mixer_parts(
        lax.rem(t, tiles_per_stream), q_ref, kc_ref, kp_ref, vc_ref, vp_ref, uc_ref, up_ref, bias_ref,
        sink_ref, cw_ref, cb_ref, lng_ref, lnb_ref, mix_ref, kw_ref, vw_ref, ue_ref, us_ref,
        chunk=chunk, tile=tile, first_tile_has_no_past=True)
    cols = lambda n: slice(n * MXU_TILE, (n + 1) * MXU_TILE)

    n_units, n_chunks, n_out = D_FF // MXU_TILE, tile // chunk, D_MODEL // MXU_TILE

    def gate_up(n):
        lo = n * MXU_TILE
        gate = _dot(xn_ref[...], wgu_ref[:, lo:lo + MXU_TILE])
        up = _dot(xn_ref[...], wgu_ref[:, D_FF + lo:D_FF + lo + MXU_TILE])
        act_ref[:, cols(n)] = (gate * jax.nn.sigmoid(gate) * up).astype(BF16)
        if n < n_chunks:
            attend(n)

    def down(n):
        x2_ref[:, cols(n)] = x2_ref[:, cols(n)] + 0.5 * _dot(act_ref[...], wd_ref[:, cols(n)])

    stage()
    for n in range(n_out):
        x2_ref[:, cols(n)] = x1_ref[:, cols(n)] + _dot(mix_ref[...], wout_ref[:, cols(n)])
    xn_ref[...] = _rms(x2_ref[...], g2_ref[...]).astype(BF16)
    for n in range(n_units):
        gate_up(n)
    for c in range(n_units, n_chunks):
        attend(c)
    for n in range(n_out):
        down(n)
    y_ref[...] = _rms(x2_ref[...], gfin_ref[...])
    shift, conv_block, n_blocks = conv
    shift()
    for r in range(n_blocks):
        conv_block(r)


def _mixer_parts(i, q_ref, kc_ref, kp_ref, vc_ref, vp_ref, uc_ref, up_ref, bias_ref, sink_ref,
                 cw_ref, cb_ref, lng_ref, lnb_ref, mix_ref, kw_ref, vw_ref, ue_ref, us_ref,
                 *, chunk, tile, first_tile_has_no_past):
    kwin_len = WINDOW + chunk
    n_bias = bias_ref.shape[0]

    def attend(c):
        low_half = lax.broadcasted_iota(jnp.int32, (chunk, LANES), 1) < HEAD_DIM
        sink = sink_ref[...]
        qc = q_ref[c * chunk:(c + 1) * chunk, :]
        zero = jnp.zeros_like(qc[:, :LANES])
        blocks = []
        for j in range(N_KV_HEADS):
            keep = low_half if j == 0 else jnp.logical_not(low_half)
            for p in range(GROUP):
                blocks.append(jnp.where(keep, qc[:, p * LANES:(p + 1) * LANES], zero))
        q_big = jnp.concatenate(blocks, axis=0)
        kwin = kw_ref[c * chunk:c * chunk + kwin_len, :]
        vwin = vw_ref[c * chunk:c * chunk + kwin_len, :]
        s = lax.dot_general(kwin, q_big, (((1,), (1,)), ((), ())), preferred_element_type=F32)
        bidx = jnp.minimum(i * (tile // chunk) + c, n_bias - 1)
        s = s + bias_ref[bidx]
        m = jnp.maximum(jnp.max(s, axis=0, keepdims=True), sink)
        e = jnp.exp(s - m)
        denom = jnp.sum(e, axis=0, keepdims=True) + jnp.exp(sink - m)
        prob = (e * (1.0 / denom)).astype(BF16)
        o = lax.dot_general(prob, vwin, (((0,), (0,)), ((), ())), preferred_element_type=F32)
        for p in range(GROUP):
            o0 = o[p * chunk:(p + 1) * chunk, :]
            o1 = o[(GROUP + p) * chunk:(GROUP + p + 1) * chunk, :]
            mix_ref[c * chunk:(c + 1) * chunk, p * LANES:(p + 1) * LANES] = (
                jnp.where(low_half, o0, o1).astype(BF16))

    def stage():
        kw_ref[0:WINDOW, :] = kp_ref[...].astype(BF16)
        kw_ref[WINDOW:, :] = kc_ref[...].astype(BF16)
        vw_ref[0:WINDOW, :] = vp_ref[...].astype(BF16)
        vw_ref[WINDOW:, :] = vc_ref[...].astype(BF16)
        u_prev = up_ref[...]
        if first_tile_has_no_past:
            u_prev = jnp.where(i > 0, u_prev, 0.0)
        ue_ref[0:CONV_PAD, :] = u_prev
        ue_ref[CONV_PAD:, :] = uc_ref[...]

    def shift():
        shifted_rows = us_ref.shape[1]
        for b in range(1, SUBLANES):
            us_ref[b - 1] = ue_ref[b:b + shifted_rows, :]

    rows = min(CONV_ROWS, tile)
    first_tap = CONV_PAD - (CONV_K - 1)

    def conv_block(r):
        acc = jnp.zeros((rows, CONV_WIDTH), F32)
        for j in range(CONV_K):
            a, b = divmod(first_tap + j, SUBLANES)
            lo = r * rows + a * SUBLANES
            taps = ue_ref[lo:lo + rows, :] if b == 0 else us_ref[b - 1, lo:lo + rows, :]
            w = cw_ref[j * SUBLANES:(j + 1) * SUBLANES, :]
            acc = acc + taps * jnp.concatenate([w] * (rows // SUBLANES), axis=0)
        y = acc + cb_ref[...]
        mu = jnp.mean(y, axis=-1, keepdims=True)
        yc = y - mu
        var = jnp.mean(yc * yc, axis=-1, keepdims=True)
        yn = yc * lax.rsqrt(var + EPS) * lng_ref[...] + lnb_ref[...]
        mix_ref[r * rows:(r + 1) * rows, ATT_WIDTH:] = (yn * jax.nn.sigmoid(yn)).astype(BF16)

    return stage, attend, (shift, conv_block, tile // rows)


def _const_spec(shape):
    return pl.BlockSpec(shape, lambda *_: (0,) * len(shape), pipeline_mode=pl.Buffered(1))


def _params(n_axes):
    return pltpu.CompilerParams(dimension_semantics=("arbitrary",) * n_axes,
                                vmem_limit_bytes=VMEM_LIMIT)


def _ffn1_proj(x, g1, wgu, wd, gmix, win, gsum, gqk):
    n = x.shape[0]
    tm = min(TOKEN_TILE, n)
    row = lambda w: pl.BlockSpec((tm, w), lambda i: (i, 0))
    return pl.pallas_call(
        _ffn1_proj_kernel,
        grid=(n // tm,),
        in_specs=[row(D_MODEL), _const_spec(g1.shape), _const_spec(wgu.shape), _const_spec(wd.shape),
                  _const_spec(gmix.shape), _const_spec(win.shape), _const_spec(gsum.shape),
                  _const_spec(gqk.shape)],
        out_specs=[row(D_MODEL), row(ATT_WIDTH), row(KV_WIDTH), row(KV_WIDTH), row(CONV_WIDTH)],
        out_shape=[jax.ShapeDtypeStruct((n, D_MODEL), F32), jax.ShapeDtypeStruct((n, ATT_WIDTH), BF16),
                   jax.ShapeDtypeStruct((n, KV_WIDTH), F32), jax.ShapeDtypeStruct((n, KV_WIDTH), F32),
                   jax.ShapeDtypeStruct((n, CONV_WIDTH), F32)],
        compiler_params=_params(1),
        name="ffn1_proj",
    )(x, g1, wgu, wd, gmix, win, gsum, gqk)


def _out_ffn2(x1, mix, wout, g2, wgu, wd, gfin):
    n = x1.shape[0]
    tm = min(TOKEN_TILE, n)
    row = lambda w: pl.BlockSpec((tm, w), lambda i: (i, 0))
    return pl.pallas_call(
        _out_ffn2_kernel,
        grid=(n // tm,),
        in_specs=[row(D_MODEL), row(D_MODEL), _const_spec(wout.shape), _const_spec(g2.shape),
                  _const_spec(wgu.shape), _const_spec(wd.shape), _const_spec(gfin.shape)],
        out_specs=row(D_MODEL),
        out_shape=jax.ShapeDtypeStruct((n, D_MODEL), F32),
        compiler_params=_params(1),
        name="out_ffn2",
    )(x1, mix, wout, g2, wgu, wd, gfin)


def _mixer_scratch(tile):
    return [pltpu.VMEM((WINDOW + tile, KV_WIDTH), BF16),
            pltpu.VMEM((WINDOW + tile, KV_WIDTH), BF16),
            pltpu.VMEM((CONV_PAD + tile, CONV_WIDTH), F32),
            pltpu.VMEM((SUBLANES - 1, CONV_PAD + tile - SUBLANES, CONV_WIDTH), F32)]


def _mixer(q, k, v, u, k_past, v_past, u_past, bias, sink_row, cw, cb, lng, lnb,
           *, n_streams, seq, chunk, tile):
    nt = seq // tile
    cur = lambda w: pl.BlockSpec((tile, w), lambda b, i: (b * nt + i, 0))
    kv_past = pl.BlockSpec((WINDOW, KV_WIDTH), lambda b, i: (b, 0))
    u_past_spec = pl.BlockSpec((CONV_PAD, CONV_WIDTH), lambda b, i: (b, 0))
    body = functools.partial(_mixer_kernel, chunk=chunk, tile=tile)
    return pl.pallas_call(
        body,
        grid=(n_streams, nt),
        in_specs=[cur(ATT_WIDTH), cur(KV_WIDTH), kv_past, cur(KV_WIDTH), kv_past, cur(CONV_WIDTH),
                  u_past_spec, _const_spec(bias.shape), _const_spec(sink_row.shape),
                  _const_spec(cw.shape), _const_spec(cb.shape), _const_spec(lng.shape),
                  _const_spec(lnb.shape)],
        out_specs=cur(D_MODEL),
        out_shape=jax.ShapeDtypeStruct((n_streams * seq, D_MODEL), BF16),
        scratch_shapes=_mixer_scratch(tile),
        compiler_params=_params(2),
        name="mixer",
    )(q, k, k_past, v, v_past, u, u_past, bias, sink_row, cw, cb, lng, lnb)


def _mixer_ffn2(q, k, v, u, bias, sink_row, cw, cb, lng, lnb, x1, wout, g2, wgu, wd, gfin,
                *, seq, chunk, tile):
    n_tiles = q.shape[0] // tile
    mix_tile = lambda s: jnp.minimum(s, n_tiles - 1)
    ffn_tile = lambda s: jnp.maximum(s - 1, 0)
    cur = lambda w: pl.BlockSpec((tile, w), lambda s: (mix_tile(s), 0))
    kv_past = pl.BlockSpec(
        (WINDOW, KV_WIDTH), lambda s: (jnp.maximum(mix_tile(s) * (tile // WINDOW) - 1, 0), 0))
    u_past = pl.BlockSpec(
        (CONV_PAD, CONV_WIDTH), lambda s: (jnp.maximum(mix_tile(s) * (tile // CONV_PAD) - 1, 0), 0))
    ffn_row = pl.BlockSpec((tile, D_MODEL), lambda s: (ffn_tile(s), 0))
    body = functools.partial(_mixer_ffn2_kernel, chunk=chunk, tile=tile, tiles_per_stream=seq // tile)
    return pl.pallas_call(
        body,
        grid=(n_tiles + 1,),
        in_specs=[cur(ATT_WIDTH), cur(KV_WIDTH), kv_past, cur(KV_WIDTH), kv_past, cur(CONV_WIDTH),
                  u_past, _const_spec(bias.shape), _const_spec(sink_row.shape),
                  _const_spec(cw.shape), _const_spec(cb.shape), _const_spec(lng.shape),
                  _const_spec(lnb.shape),
                  ffn_row, _const_spec(wout.shape), _const_spec(g2.shape), _const_spec(wgu.shape),
                  _const_spec(wd.shape), _const_spec(gfin.shape)],
        out_specs=ffn_row,
        out_shape=jax.ShapeDtypeStruct((q.shape[0], D_MODEL), F32),
        scratch_shapes=[pltpu.VMEM((tile, D_MODEL), BF16), pltpu.VMEM((tile, D_MODEL), F32),
                        pltpu.VMEM((tile, D_MODEL), BF16), pltpu.VMEM((tile, D_FF), BF16)]
        + _mixer_scratch(tile),
        compiler_params=_params(1),
        name="mixer_ffn2",
    )(q, k, k, v, v, u, u, bias, sink_row, cw, cb, lng, lnb, x1, wout, g2, wgu, wd, gfin)


def _rel_bucket(rel):
    nb = NUM_BUCKETS // 2
    max_exact = nb // 2
    ret = np.where(rel > 0, nb, 0)
    n = np.abs(rel)
    nf = np.maximum(n, 1).astype(np.float32)
    large = max_exact + (np.log(nf / np.float32(max_exact)) / np.float32(math.log(MAX_DISTANCE / max_exact))
                         * np.float32(nb - max_exact)).astype(np.int32)
    large = np.minimum(large, nb - 1)
    return (ret + np.where(n < max_exact, n, large)).astype(np.int32)


def _bias_cols(table, qpos, kpos):
    bucket = _rel_bucket(kpos[:, None] - qpos[None, :])
    onehot = jnp.asarray(bucket[:, :, None] == np.arange(NUM_BUCKETS), dtype=F32)
    b = jnp.einsum("kqb,bh->khq", onehot, table.astype(F32), precision=lax.Precision.HIGHEST)
    return b.reshape(kpos.shape[0], N_HEADS * qpos.shape[0])


def kernel(x_prompt, x_sample, cache_k, cache_v, state_conv, rel_bias_table, ffn1_norm, ffn1_w_gu,
           ffn1_w_down, mix_norm, w_in, q_norm, k_norm, sinks, conv_w, conv_b, conv_ln_g, conv_ln_b,
           w_out, ffn2_norm, ffn2_w_gu, ffn2_w_down, final_norm):
    B, S, _ = x_prompt.shape
    DB, DS, _ = x_sample.shape
    R = cache_k.shape[2]
    past_len = R
    assert ffn1_norm.shape[0] == 1 and R == WINDOW and S % MIX_TILE == 0 and DS % 8 == 0
    l = 0
    row = lambda a: a.reshape(1, -1).astype(F32)

    pair = lambda a, axis: jnp.swapaxes(
        a.reshape(a.shape[:axis] + (N_KV_HEADS, GROUP, HEAD_DIM) + a.shape[axis + 1:]),
        axis, axis + 1).reshape(a.shape)
    win = jnp.concatenate([pair(w_in[l][:, :ATT_WIDTH], 1), w_in[l][:, ATT_WIDTH:]], axis=1).astype(BF16)
    wout = jnp.concatenate([pair(w_out[l][:ATT_WIDTH], 0), w_out[l][ATT_WIDTH:]], axis=0).astype(BF16)
    gqk = jnp.concatenate([jnp.tile(q_norm[l], N_HEADS) * (HEAD_DIM ** -0.5),
                           jnp.tile(k_norm[l], N_KV_HEADS)]).reshape(1, QK_WIDTH).astype(F32)
    head_of_col = np.arange(NORM_GROUP) // HEAD_DIM
    gsum = jnp.asarray(head_of_col[:, None] == head_of_col[None, :], dtype=BF16)
    wgu1, wd1 = ffn1_w_gu[l].astype(BF16), ffn1_w_down[l].astype(BF16)
    wgu2, wd2 = ffn2_w_gu[l].astype(BF16), ffn2_w_down[l].astype(BF16)
    cw = jnp.broadcast_to(conv_w[l].astype(F32)[:, None, :], (CONV_K, SUBLANES, CONV_WIDTH))
    cw, cb = cw.reshape(CONV_K * SUBLANES, CONV_WIDTH), row(conv_b[l])
    lng, lnb = row(conv_ln_g[l]), row(conv_ln_b[l])

    band = WINDOW + CHUNK
    bias_p = _bias_cols(rel_bias_table, np.arange(CHUNK), np.arange(band) - WINDOW)
    key = np.arange(band)[:, None]
    neg = lambda first_valid: jnp.where(key >= first_valid, bias_p, -jnp.inf)
    bias_prompt = jnp.stack([neg(WINDOW), neg(WINDOW - CHUNK), bias_p])
    qpos_s = past_len + np.arange(DS)
    kpos_s = np.concatenate([past_len - R + np.arange(R), qpos_s])
    bias_sample = _bias_cols(rel_bias_table, qpos_s, kpos_s)[None]
    sink_row = lambda n: jnp.broadcast_to(sinks[l].astype(F32)[:, None], (N_HEADS, n)).reshape(1, N_HEADS * n)

    def layer(x, n_streams, seq, chunk, tile, past):
        x1, q, k, v, u = _ffn1_proj(x, row(ffn1_norm[l]), wgu1, wd1, row(mix_norm[l]), win, gsum, gqk)
        if past is None:
            y = _mixer_ffn2(q, k, v, u, bias_prompt, sink_row(chunk), cw, cb, lng, lnb, x1, wout,
                            row(ffn2_norm[l]), wgu2, wd2, row(final_norm[l]),
                            seq=seq, chunk=chunk, tile=tile)
        else:
            mix = _mixer(q, k, v, u, *past, bias_sample, sink_row(chunk), cw, cb, lng, lnb,
                         n_streams=n_streams, seq=seq, chunk=chunk, tile=tile)
            y = _out_ffn2(x1, mix, wout, row(ffn2_norm[l]), wgu2, wd2, row(final_norm[l]))
        return y, k, v, u

    yp, kp, vp, up = layer(x_prompt.reshape(B * S, D_MODEL), B, S, CHUNK, MIX_TILE, None)
    ck = cache_k[l].reshape(DB * R, KV_WIDTH).astype(F32)
    cv = cache_v[l].reshape(DB * R, KV_WIDTH).astype(F32)
    sc = jnp.pad(state_conv[l].astype(F32), ((0, 0), (CONV_PAD - (CONV_K - 1), 0), (0, 0)))
    ys, ks, vs, us = layer(x_sample.reshape(DB * DS, D_MODEL), DB, DS, DS, DS,
                           (ck, cv, sc.reshape(DB * CONV_PAD, CONV_WIDTH)))

    RP = min(WINDOW, S)
    heads = lambda a, n, t: a.reshape(n, t, N_KV_HEADS, HEAD_DIM)
    tail = lambda a, n: a.reshape(B, S, a.shape[-1])[:, S - n:]
    new_k_prompt = heads(tail(kp, RP), B, RP)[None]
    new_v_prompt = heads(tail(vp, RP), B, RP)[None]
    new_conv_prompt = tail(up, CONV_K - 1)[None]
    new_k_sample = jnp.concatenate([cache_k[l].astype(F32), heads(ks, DB, DS)], axis=1)[:, DS:][None]
    new_v_sample = jnp.concatenate([cache_v[l].astype(F32), heads(vs, DB, DS)], axis=1)[:, DS:][None]
    new_conv_sample = jnp.concatenate([state_conv[l].astype(F32), us.reshape(DB, DS, CONV_WIDTH)],
                                      axis=1)[:, DS:][None]
    return (yp.reshape(B, S, D_MODEL), ys.reshape(DB, DS, D_MODEL), new_k_prompt, new_v_prompt,
            new_conv_prompt, new_k_sample, new_v_sample, new_conv_sample)
```

```python
import functools
import math

import numpy as np
import jax
import jax.numpy as jnp
from jax import lax
from jax.experimental import pallas as pl
from jax.experimental.pallas import tpu as pltpu

D_MODEL = 1024
CHUNK = 64
HEAD_DIM = 64
N_HEADS = 8
N_KV_HEADS = 2
GROUP = N_HEADS // N_KV_HEADS
ATT_WIDTH = N_HEADS * HEAD_DIM
KV_WIDTH = N_KV_HEADS * HEAD_DIM
CONV_WIDTH = D_MODEL - ATT_WIDTH
CONV_K = 31
WINDOW = 128
IN_WIDTH = ATT_WIDTH + 2 * KV_WIDTH + 2 * CONV_WIDTH
QK_WIDTH = ATT_WIDTH + KV_WIDTH
D_FF = 2816
NUM_BUCKETS = 32
MAX_DISTANCE = 128
EPS = 1e-6

LANES = 128
CONV_PAD = 32
FF_SPLITS = (0, 1536, D_FF)
SUBLANES = 8
MXU_TILE = 256
NORM_GROUP = MXU_TILE
CONV_ROWS = 32
TOKEN_TILE = 1024
MIX_TILE = 512
VMEM_LIMIT = 56 * 1024 * 1024

F32 = jnp.float32
BF16 = jnp.bfloat16


def _dot(a, b):
    return jnp.dot(a, b, preferred_element_type=F32)


def _rms(x, g):
    ms = jnp.mean(x * x, axis=-1, keepdims=True)
    return x * lax.rsqrt(ms + EPS) * g


def _swiglu(xn, wgu_ref, wd_ref):
    acc = None
    for lo, hi in zip(FF_SPLITS[:-1], FF_SPLITS[1:]):
        gate = _dot(xn, wgu_ref[:, lo:hi])
        up = _dot(xn, wgu_ref[:, D_FF + lo:D_FF + hi])
        act = (gate * jax.nn.sigmoid(gate) * up).astype(BF16)
        part = _dot(act, wd_ref[lo:hi, :])
        acc = part if acc is None else acc + part
    return acc


def _ffn1_proj_kernel(x_ref, g1_ref, wgu_ref, wd_ref, gmix_ref, win_ref, gsum_ref, gqk_ref,
                      x1_ref, q_ref, k_ref, v_ref, u_ref):
    x = x_ref[...]
    x1 = x + 0.5 * _swiglu(_rms(x, g1_ref[...]).astype(BF16), wgu_ref, wd_ref)
    x1_ref[...] = x1
    z = _dot(_rms(x1, gmix_ref[...]).astype(BF16), win_ref[...])
    qk = z[:, :QK_WIDTH]
    sq = (qk * qk).astype(BF16)
    ss = jnp.concatenate(
        [_dot(sq[:, lo:lo + NORM_GROUP], gsum_ref[...]) for lo in range(0, ATT_WIDTH, NORM_GROUP)]
        + [_dot(sq[:, ATT_WIDTH:], gsum_ref[:KV_WIDTH, :KV_WIDTH])], axis=1)
    qkn = qk * lax.rsqrt(ss * (1.0 / HEAD_DIM) + EPS) * gqk_ref[...]
    q_ref[...] = qkn[:, :ATT_WIDTH].astype(BF16)
    k_ref[...] = qkn[:, ATT_WIDTH:]
    v_ref[...] = z[:, QK_WIDTH:QK_WIDTH + KV_WIDTH]
    a = z[:, QK_WIDTH + KV_WIDTH:QK_WIDTH + KV_WIDTH + CONV_WIDTH]
    g = z[:, QK_WIDTH + KV_WIDTH + CONV_WIDTH:]
    u_ref[...] = a * jax.nn.sigmoid(g)


def _out_ffn2_kernel(x1_ref, mix_ref, wout_ref, g2_ref, wgu_ref, wd_ref, gfin_ref, y_ref):
    x2 = x1_ref[...] + _dot(mix_ref[...], wout_ref[...])
    x3 = x2 + 0.5 * _swiglu(_rms(x2, g2_ref[...]).astype(BF16), wgu_ref, wd_ref)
    y_ref[...] = _rms(x3, gfin_ref[...])


def _mixer_kernel(*refs, chunk, tile):
    stage, attend, conv = _mixer_parts(pl.program_id(1), *refs, chunk=chunk, tile=tile,
                                       first_tile_has_no_past=False)
    stage()
    scores, values = attend
    for c in range(tile // chunk):
        values(c, scores(c))
    shift, conv_block, n_blocks = conv
    shift()
    for r in range(n_blocks):
        conv_block(r)


def _mixer_ffn2_kernel(q_ref, kc_ref, kp_ref, vc_ref, vp_ref, uc_ref, up_ref, bias_ref, sink_ref,
                       cw_ref, cb_ref, lng_ref, lnb_ref,
                       x1_ref, wout_ref, g2_ref, wgu_ref, wd_ref, gfin_ref,
                       y_ref, mix_ref, x2_ref, xn_ref, act_ref, kw_ref, vw_ref, ue_ref, us_ref,
                       *, chunk, tile, tiles_per_stream):
    s = pl.program_id(0)

    @pl.when(s == 0)
    def _():
        mix_ref[...] = jnp.zeros_like(mix_ref)

    t = jnp.minimum(s, pl.num_programs(0) - 2)
    stage, attend, conv = _mixer_parts(
        lax.rem(t, tiles_per_stream), q_ref, kc_ref, kp_ref, vc_ref, vp_ref, uc_ref, up_ref, bias_ref,
        sink_ref, cw_ref, cb_ref, lng_ref, lnb_ref, mix_ref, kw_ref, vw_ref, ue_ref, us_ref,
        chunk=chunk, tile=tile, first_tile_has_no_past=True)
    cols = lambda n: slice(n * MXU_TILE, (n + 1) * MXU_TILE)

    n_units, n_chunks, n_out = D_FF // MXU_TILE, tile // chunk, D_MODEL // MXU_TILE

    def gate_up(n):
        lo = n * MXU_TILE
        gate = _dot(xn_ref[...], wgu_ref[:, lo:lo + MXU_TILE])
        up = _dot(xn_ref[...], wgu_ref[:, D_FF + lo:D_FF + lo + MXU_TILE])
        act_ref[:, cols(n)] = (gate * jax.nn.sigmoid(gate) * up).astype(BF16)

    def down(n):
        x2_ref[:, cols(n)] = x2_ref[:, cols(n)] + 0.5 * _dot(act_ref[...], wd_ref[:, cols(n)])

    stage()
    for n in range(n_out):
        x2_ref[:, cols(n)] = x1_ref[:, cols(n)] + _dot(mix_ref[...], wout_ref[:, cols(n)])
    xn_ref[...] = _rms(x2_ref[...], g2_ref[...]).astype(BF16)
    scores, values = attend
    units = ([functools.partial(gate_up, n) for n in range(n_units)]
             + [functools.partial(down, n) for n in range(n_out)])
    prob = None
    for n, unit in enumerate(units):
        unit()
        if prob is not None:
            values(n - 1, prob)
        prob = scores(n) if n < n_chunks else None
    assert n_chunks < len(units)
    y_ref[...] = _rms(x2_ref[...], gfin_ref[...])
    shift, conv_block, n_blocks = conv
    shift()
    for r in range(n_blocks):
        conv_block(r)


def _mixer_parts(i, q_ref, kc_ref, kp_ref, vc_ref, vp_ref, uc_ref, up_ref, bias_ref, sink_ref,
                 cw_ref, cb_ref, lng_ref, lnb_ref, mix_ref, kw_ref, vw_ref, ue_ref, us_ref,
                 *, chunk, tile, first_tile_has_no_past):
    kwin_len = WINDOW + chunk
    n_bias = bias_ref.shape[0]

    def scores(c):
        low_half = lax.broadcasted_iota(jnp.int32, (chunk, LANES), 1) < HEAD_DIM
        sink = sink_ref[...]
        qc = q_ref[c * chunk:(c + 1) * chunk, :]
        zero = jnp.zeros_like(qc[:, :LANES])
        blocks = []
        for j in range(N_KV_HEADS):
            keep = low_half if j == 0 else jnp.logical_not(low_half)
            for p in range(GROUP):
                blocks.append(jnp.where(keep, qc[:, p * LANES:(p + 1) * LANES], zero))
        q_big = jnp.concatenate(blocks, axis=0)
        kwin = kw_ref[c * chunk:c * chunk + kwin_len, :]
        s = lax.dot_general(kwin, q_big, (((1,), (1,)), ((), ())), preferred_element_type=F32)
        bidx = jnp.minimum(i * (tile // chunk) + c, n_bias - 1)
        s = s + bias_ref[bidx]
        m = jnp.maximum(jnp.max(s, axis=0, keepdims=True), sink)
        e = jnp.exp(s - m)
        denom = jnp.sum(e, axis=0, keepdims=True) + jnp.exp(sink - m)
        return (e * (1.0 / denom)).astype(BF16)

    def values(c, prob):
        low_half = lax.broadcasted_iota(jnp.int32, (chunk, LANES), 1) < HEAD_DIM
        vwin = vw_ref[c * chunk:c * chunk + kwin_len, :]
        o = lax.dot_general(prob, vwin, (((0,), (0,)), ((), ())), preferred_element_type=F32)
        for p in range(GROUP):
            o0 = o[p * chunk:(p + 1) * chunk, :]
            o1 = o[(GROUP + p) * chunk:(GROUP + p + 1) * chunk, :]
            mix_ref[c * chunk:(c + 1) * chunk, p * LANES:(p + 1) * LANES] = (
                jnp.where(low_half, o0, o1).astype(BF16))

    def stage():
        kw_ref[0:WINDOW, :] = kp_ref[...].astype(BF16)
        kw_ref[WINDOW:, :] = kc_ref[...].astype(BF16)
        vw_ref[0:WINDOW, :] = vp_ref[...].astype(BF16)
        vw_ref[WINDOW:, :] = vc_ref[...].astype(BF16)
        u_prev = up_ref[...]
        if first_tile_has_no_past:
            u_prev = jnp.where(i > 0, u_prev, 0.0)
        ue_ref[0:CONV_PAD, :] = u_prev
        ue_ref[CONV_PAD:, :] = uc_ref[...]

    def shift():
        shifted_rows = us_ref.shape[1]
        for b in range(1, SUBLANES):
            us_ref[b - 1] = ue_ref[b:b + shifted_rows, :]

    rows = min(CONV_ROWS, tile)
    first_tap = CONV_PAD - (CONV_K - 1)

    def conv_block(r):
        acc = jnp.zeros((rows, CONV_WIDTH), F32)
        for j in range(CONV_K):
            a, b = divmod(first_tap + j, SUBLANES)
            lo = r * rows + a * SUBLANES
            taps = ue_ref[lo:lo + rows, :] if b == 0 else us_ref[b - 1, lo:lo + rows, :]
            w = cw_ref[j * SUBLANES:(j + 1) * SUBLANES, :]
            acc = acc + taps * jnp.concatenate([w] * (rows // SUBLANES), axis=0)
        y = acc + cb_ref[...]
        mu = jnp.mean(y, axis=-1, keepdims=True)
        yc = y - mu
        var = jnp.mean(yc * yc, axis=-1, keepdims=True)
        yn = yc * lax.rsqrt(var + EPS) * lng_ref[...] + lnb_ref[...]
        mix_ref[r * rows:(r + 1) * rows, ATT_WIDTH:] = (yn * jax.nn.sigmoid(yn)).astype(BF16)

    return stage, (scores, values), (shift, conv_block, tile // rows)


def _const_spec(shape):
    return pl.BlockSpec(shape, lambda *_: (0,) * len(shape), pipeline_mode=pl.Buffered(1))


def _params(n_axes):
    return pltpu.CompilerParams(dimension_semantics=("arbitrary",) * n_axes,
                                vmem_limit_bytes=VMEM_LIMIT)


def _ffn1_proj(x, g1, wgu, wd, gmix, win, gsum, gqk):
    n = x.shape[0]
    tm = min(TOKEN_TILE, n)
    row = lambda w: pl.BlockSpec((tm, w), lambda i: (i, 0))
    return pl.pallas_call(
        _ffn1_proj_kernel,
        grid=(n // tm,),
        in_specs=[row(D_MODEL), _const_spec(g1.shape), _const_spec(wgu.shape), _const_spec(wd.shape),
                  _const_spec(gmix.shape), _const_spec(win.shape), _const_spec(gsum.shape),
                  _const_spec(gqk.shape)],
        out_specs=[row(D_MODEL), row(ATT_WIDTH), row(KV_WIDTH), row(KV_WIDTH), row(CONV_WIDTH)],
        out_shape=[jax.ShapeDtypeStruct((n, D_MODEL), F32), jax.ShapeDtypeStruct((n, ATT_WIDTH), BF16),
                   jax.ShapeDtypeStruct((n, KV_WIDTH), F32), jax.ShapeDtypeStruct((n, KV_WIDTH), F32),
                   jax.ShapeDtypeStruct((n, CONV_WIDTH), F32)],
        compiler_params=_params(1),
        name="ffn1_proj",
    )(x, g1, wgu, wd, gmix, win, gsum, gqk)


def _out_ffn2(x1, mix, wout, g2, wgu, wd, gfin):
    n = x1.shape[0]
    tm = min(TOKEN_TILE, n)
    row = lambda w: pl.BlockSpec((tm, w), lambda i: (i, 0))
    return pl.pallas_call(
        _out_ffn2_kernel,
        grid=(n // tm,),
        in_specs=[row(D_MODEL), row(D_MODEL), _const_spec(wout.shape), _const_spec(g2.shape),
                  _const_spec(wgu.shape), _const_spec(wd.shape), _const_spec(gfin.shape)],
        out_specs=row(D_MODEL),
        out_shape=jax.ShapeDtypeStruct((n, D_MODEL), F32),
        compiler_params=_params(1),
        name="out_ffn2",
    )(x1, mix, wout, g2, wgu, wd, gfin)


def _mixer_scratch(tile):
    return [pltpu.VMEM((WINDOW + tile, KV_WIDTH), BF16),
            pltpu.VMEM((WINDOW + tile, KV_WIDTH), BF16),
            pltpu.VMEM((CONV_PAD + tile, CONV_WIDTH), F32),
            pltpu.VMEM((SUBLANES - 1, CONV_PAD + tile - SUBLANES, CONV_WIDTH), F32)]


def _mixer(q, k, v, u, k_past, v_past, u_past, bias, sink_row, cw, cb, lng, lnb,
           *, n_streams, seq, chunk, tile):
    nt = seq // tile
    cur = lambda w: pl.BlockSpec((tile, w), lambda b, i: (b * nt + i, 0))
    kv_past = pl.BlockSpec((WINDOW, KV_WIDTH), lambda b, i: (b, 0))
    u_past_spec = pl.BlockSpec((CONV_PAD, CONV_WIDTH), lambda b, i: (b, 0))
    body = functools.partial(_mixer_kernel, chunk=chunk, tile=tile)
    return pl.pallas_call(
        body,
        grid=(n_streams, nt),
        in_specs=[cur(ATT_WIDTH), cur(KV_WIDTH), kv_past, cur(KV_WIDTH), kv_past, cur(CONV_WIDTH),
                  u_past_spec, _const_spec(bias.shape), _const_spec(sink_row.shape),
                  _const_spec(cw.shape), _const_spec(cb.shape), _const_spec(lng.shape),
                  _const_spec(lnb.shape)],
        out_specs=cur(D_MODEL),
        out_shape=jax.ShapeDtypeStruct((n_streams * seq, D_MODEL), BF16),
        scratch_shapes=_mixer_scratch(tile),
        compiler_params=_params(2),
        name="mixer",
    )(q, k, k_past, v, v_past, u, u_past, bias, sink_row, cw, cb, lng, lnb)


def _mixer_ffn2(q, k, v, u, bias, sink_row, cw, cb, lng, lnb, x1, wout, g2, wgu, wd, gfin,
                *, seq, chunk, tile):
    n_tiles = q.shape[0] // tile
    mix_tile = lambda s: jnp.minimum(s, n_tiles - 1)
    ffn_tile = lambda s: jnp.maximum(s - 1, 0)
    cur = lambda w: pl.BlockSpec((tile, w), lambda s: (mix_tile(s), 0))
    kv_past = pl.BlockSpec(
        (WINDOW, KV_WIDTH), lambda s: (jnp.maximum(mix_tile(s) * (tile // WINDOW) - 1, 0), 0))
    u_past = pl.BlockSpec(
        (CONV_PAD, CONV_WIDTH), lambda s: (jnp.maximum(mix_tile(s) * (tile // CONV_PAD) - 1, 0), 0))
    ffn_row = pl.BlockSpec((tile, D_MODEL), lambda s: (ffn_tile(s), 0))
    body = functools.partial(_mixer_ffn2_kernel, chunk=chunk, tile=tile, tiles_per_stream=seq // tile)
    return pl.pallas_call(
        body,
        grid=(n_tiles + 1,),
        in_specs=[cur(ATT_WIDTH), cur(KV_WIDTH), kv_past, cur(KV_WIDTH), kv_past, cur(CONV_WIDTH),
                  u_past, _const_spec(bias.shape), _const_spec(sink_row.shape),
                  _const_spec(cw.shape), _const_spec(cb.shape), _const_spec(lng.shape),
                  _const_spec(lnb.shape),
                  ffn_row, _const_spec(wout.shape), _const_spec(g2.shape), _const_spec(wgu.shape),
                  _const_spec(wd.shape), _const_spec(gfin.shape)],
        out_specs=ffn_row,
        out_shape=jax.ShapeDtypeStruct((q.shape[0], D_MODEL), F32),
        scratch_shapes=[pltpu.VMEM((tile, D_MODEL), BF16), pltpu.VMEM((tile, D_MODEL), F32),
                        pltpu.VMEM((tile, D_MODEL), BF16), pltpu.VMEM((tile, D_FF), BF16)]
        + _mixer_scratch(tile),
        compiler_params=_params(1),
        name="mixer_ffn2",
    )(q, k, k, v, v, u, u, bias, sink_row, cw, cb, lng, lnb, x1, wout, g2, wgu, wd, gfin)


def _rel_bucket(rel):
    nb = NUM_BUCKETS // 2
    max_exact = nb // 2
    ret = np.where(rel > 0, nb, 0)
    n = np.abs(rel)
    nf = np.maximum(n, 1).astype(np.float32)
    large = max_exact + (np.log(nf / np.float32(max_exact)) / np.float32(math.log(MAX_DISTANCE / max_exact))
                         * np.float32(nb - max_exact)).astype(np.int32)
    large = np.minimum(large, nb - 1)
    return (ret + np.where(n < max_exact, n, large)).astype(np.int32)


def _bias_cols(table, qpos, kpos):
    bucket = _rel_bucket(kpos[:, None] - qpos[None, :])
    onehot = jnp.asarray(bucket[:, :, None] == np.arange(NUM_BUCKETS), dtype=F32)
    b = jnp.einsum("kqb,bh->khq", onehot, table.astype(F32), precision=lax.Precision.HIGHEST)
    return b.reshape(kpos.shape[0], N_HEADS * qpos.shape[0])


def kernel(x_prompt, x_sample, cache_k, cache_v, state_conv, rel_bias_table, ffn1_norm, ffn1_w_gu,
           ffn1_w_down, mix_norm, w_in, q_norm, k_norm, sinks, conv_w, conv_b, conv_ln_g, conv_ln_b,
           w_out, ffn2_norm, ffn2_w_gu, ffn2_w_down, final_norm):
    B, S, _ = x_prompt.shape
    DB, DS, _ = x_sample.shape
    R = cache_k.shape[2]
    past_len = R
    assert ffn1_norm.shape[0] == 1 and R == WINDOW and S % MIX_TILE == 0 and DS % 8 == 0
    l = 0
    row = lambda a: a.reshape(1, -1).astype(F32)

    pair = lambda a, axis: jnp.swapaxes(
        a.reshape(a.shape[:axis] + (N_KV_HEADS, GROUP, HEAD_DIM) + a.shape[axis + 1:]),
        axis, axis + 1).reshape(a.shape)
    win = jnp.concatenate([pair(w_in[l][:, :ATT_WIDTH], 1), w_in[l][:, ATT_WIDTH:]], axis=1).astype(BF16)
    wout = jnp.concatenate([pair(w_out[l][:ATT_WIDTH], 0), w_out[l][ATT_WIDTH:]], axis=0).astype(BF16)
    gqk = jnp.concatenate([jnp.tile(q_norm[l], N_HEADS) * (HEAD_DIM ** -0.5),
                           jnp.tile(k_norm[l], N_KV_HEADS)]).reshape(1, QK_WIDTH).astype(F32)
    head_of_col = np.arange(NORM_GROUP) // HEAD_DIM
    gsum = jnp.asarray(head_of_col[:, None] == head_of_col[None, :], dtype=BF16)
    wgu1, wd1 = ffn1_w_gu[l].astype(BF16), ffn1_w_down[l].astype(BF16)
    wgu2, wd2 = ffn2_w_gu[l].astype(BF16), ffn2_w_down[l].astype(BF16)
    cw = jnp.broadcast_to(conv_w[l].astype(F32)[:, None, :], (CONV_K, SUBLANES, CONV_WIDTH))
    cw, cb = cw.reshape(CONV_K * SUBLANES, CONV_WIDTH), row(conv_b[l])
    lng, lnb = row(conv_ln_g[l]), row(conv_ln_b[l])

    band = WINDOW + CHUNK
    bias_p = _bias_cols(rel_bias_table, np.arange(CHUNK), np.arange(band) - WINDOW)
    key = np.arange(band)[:, None]
    neg = lambda first_valid: jnp.where(key >= first_valid, bias_p, -jnp.inf)
    bias_prompt = jnp.stack([neg(WINDOW), neg(WINDOW - CHUNK), bias_p])
    qpos_s = past_len + np.arange(DS)
    kpos_s = np.concatenate([past_len - R + np.arange(R), qpos_s])
    bias_sample = _bias_cols(rel_bias_table, qpos_s, kpos_s)[None]
    sink_row = lambda n: jnp.broadcast_to(sinks[l].astype(F32)[:, None], (N_HEADS, n)).reshape(1, N_HEADS * n)

    def layer(x, n_streams, seq, chunk, tile, past):
        x1, q, k, v, u = _ffn1_proj(x, row(ffn1_norm[l]), wgu1, wd1, row(mix_norm[l]), win, gsum, gqk)
        if past is None:
            y = _mixer_ffn2(q, k, v, u, bias_prompt, sink_row(chunk), cw, cb, lng, lnb, x1, wout,
                            row(ffn2_norm[l]), wgu2, wd2, row(final_norm[l]),
                            seq=seq, chunk=chunk, tile=tile)
        else:
            mix = _mixer(q, k, v, u, *past, bias_sample, sink_row(chunk), cw, cb, lng, lnb,
                         n_streams=n_streams, seq=seq, chunk=chunk, tile=tile)
            y = _out_ffn2(x1, mix, wout, row(ffn2_norm[l]), wgu2, wd2, row(final_norm[l]))
        return y, k, v, u

    yp, kp, vp, up = layer(x_prompt.reshape(B * S, D_MODEL), B, S, CHUNK, MIX_TILE, None)
    ck = cache_k[l].reshape(DB * R, KV_WIDTH).astype(F32)
    cv = cache_v[l].reshape(DB * R, KV_WIDTH).astype(F32)
    sc = jnp.pad(state_conv[l].astype(F32), ((0, 0), (CONV_PAD - (CONV_K - 1), 0), (0, 0)))
    ys, ks, vs, us = layer(x_sample.reshape(DB * DS, D_MODEL), DB, DS, DS, DS,
                           (ck, cv, sc.reshape(DB * CONV_PAD, CONV_WIDTH)))

    RP = min(WINDOW, S)
    heads = lambda a, n, t: a.reshape(n, t, N_KV_HEADS, HEAD_DIM)
    tail = lambda a, n: a.reshape(B, S, a.shape[-1])[:, S - n:]
    new_k_prompt = heads(tail(kp, RP), B, RP)[None]
    new_v_prompt = heads(tail(vp, RP), B, RP)[None]
    new_conv_prompt = tail(up, CONV_K - 1)[None]
    new_k_sample = jnp.concatenate([cache_k[l].astype(F32), heads(ks, DB, DS)], axis=1)[:, DS:][None]
    new_v_sample = jnp.concatenate([cache_v[l].astype(F32), heads(vs, DB, DS)], axis=1)[:, DS:][None]
    new_conv_sample = jnp.concatenate([state_conv[l].astype(F32), us.reshape(DB, DS, CONV_WIDTH)],
                                      axis=1)[:, DS:][None]
    return (yp.reshape(B, S, D_MODEL), ys.reshape(DB, DS, D_MODEL), new_k_prompt, new_v_prompt,
            new_conv_prompt, new_k_sample, new_v_sample, new_conv_sample)
```

```python
import functools
import math

import numpy as np
import jax
import jax.numpy as jnp
from jax import lax
from jax.experimental import pallas as pl
from jax.experimental.pallas import tpu as pltpu

D_MODEL = 1024
CHUNK = 64
HEAD_DIM = 64
N_HEADS = 8
N_KV_HEADS = 2
GROUP = N_HEADS // N_KV_HEADS
ATT_WIDTH = N_HEADS * HEAD_DIM
KV_WIDTH = N_KV_HEADS * HEAD_DIM
CONV_WIDTH = D_MODEL - ATT_WIDTH
CONV_K = 31
WINDOW = 128
IN_WIDTH = ATT_WIDTH + 2 * KV_WIDTH + 2 * CONV_WIDTH
QK_WIDTH = ATT_WIDTH + KV_WIDTH
D_FF = 2816
NUM_BUCKETS = 32
MAX_DISTANCE = 128
EPS = 1e-6

LANES = 128
CONV_PAD = 32
FF_SPLITS = (0, 1536, D_FF)
SUBLANES = 8
MXU_TILE = 256
NORM_GROUP = MXU_TILE
CONV_ROWS = 32
CONV_GROUPS = CONV_WIDTH // LANES
TOKEN_TILE = 1024
MIX_TILE = 512
VMEM_LIMIT = 56 * 1024 * 1024

F32 = jnp.float32
BF16 = jnp.bfloat16


def _dot(a, b):
    return jnp.dot(a, b, preferred_element_type=F32)


def _rms(x, g):
    ms = jnp.mean(x * x, axis=-1, keepdims=True)
    return x * lax.rsqrt(ms + EPS) * g


def _swiglu(xn, wgu_ref, wd_ref):
    acc = None
    for lo, hi in zip(FF_SPLITS[:-1], FF_SPLITS[1:]):
        gate = _dot(xn, wgu_ref[:, lo:hi])
        up = _dot(xn, wgu_ref[:, D_FF + lo:D_FF + hi])
        act = (gate * jax.nn.sigmoid(gate) * up).astype(BF16)
        part = _dot(act, wd_ref[lo:hi, :])
        acc = part if acc is None else acc + part
    return acc


def _ffn1_proj_kernel(x_ref, g1_ref, wgu_ref, wd_ref, gmix_ref, win_ref, gsum_ref, gqk_ref,
                      x1_ref, q_ref, k_ref, v_ref, u_ref):
    x = x_ref[...]
    x1 = x + 0.5 * _swiglu(_rms(x, g1_ref[...]).astype(BF16), wgu_ref, wd_ref)
    x1_ref[...] = x1
    z = _dot(_rms(x1, gmix_ref[...]).astype(BF16), win_ref[...])
    qk = z[:, :QK_WIDTH]
    sq = (qk * qk).astype(BF16)
    ss = jnp.concatenate(
        [_dot(sq[:, lo:lo + NORM_GROUP], gsum_ref[...]) for lo in range(0, ATT_WIDTH, NORM_GROUP)]
        + [_dot(sq[:, ATT_WIDTH:], gsum_ref[:KV_WIDTH, :KV_WIDTH])], axis=1)
    qkn = qk * lax.rsqrt(ss * (1.0 / HEAD_DIM) + EPS) * gqk_ref[...]
    q_ref[...] = qkn[:, :ATT_WIDTH].astype(BF16)
    k_ref[...] = qkn[:, ATT_WIDTH:]
    v_ref[...] = z[:, QK_WIDTH:QK_WIDTH + KV_WIDTH]
    a = z[:, QK_WIDTH + KV_WIDTH:QK_WIDTH + KV_WIDTH + CONV_WIDTH]
    g = z[:, QK_WIDTH + KV_WIDTH + CONV_WIDTH:]
    u = a * jax.nn.sigmoid(g)
    for grp in range(CONV_GROUPS):
        u_ref[pl.ds(grp, u.shape[0], stride=CONV_GROUPS), :] = u[:, grp * LANES:(grp + 1) * LANES]


def _out_ffn2_kernel(x1_ref, mix_ref, wout_ref, g2_ref, wgu_ref, wd_ref, gfin_ref, y_ref):
    x2 = x1_ref[...] + _dot(mix_ref[...], wout_ref[...])
    x3 = x2 + 0.5 * _swiglu(_rms(x2, g2_ref[...]).astype(BF16), wgu_ref, wd_ref)
    y_ref[...] = _rms(x3, gfin_ref[...])


def _mixer_kernel(*refs, chunk, tile):
    stage, attend, conv = _mixer_parts(pl.program_id(1), *refs, chunk=chunk, tile=tile,
                                       first_tile_has_no_past=False)
    stage()
    scores, values = attend
    for c in range(tile // chunk):
        values(c, scores(c))
    shift, conv_block, n_blocks = conv
    shift()
    for r in range(n_blocks):
        conv_block(r)


def _mixer_ffn2_kernel(q_ref, kc_ref, kp_ref, vc_ref, vp_ref, uc_ref, up_ref, bias_ref, sink_ref,
                       cw_ref, cb_ref, lng_ref, lnb_ref,
                       x1_ref, wout_ref, g2_ref, wgu_ref, wd_ref, gfin_ref,
                       y_ref, mix_ref, x2_ref, xn_ref, act_ref, kw_ref, vw_ref, ue_ref, us_ref, ys_ref,
                       *, chunk, tile, tiles_per_stream):
    s = pl.program_id(0)

    @pl.when(s == 0)
    def _():
        mix_ref[...] = jnp.zeros_like(mix_ref)

    t = jnp.minimum(s, pl.num_programs(0) - 2)
    stage, attend, conv = _mixer_parts(
        lax.rem(t, tiles_per_stream), q_ref, kc_ref, kp_ref, vc_ref, vp_ref, uc_ref, up_ref, bias_ref,
        sink_ref, cw_ref, cb_ref, lng_ref, lnb_ref, mix_ref, kw_ref, vw_ref, ue_ref, us_ref, ys_ref,
        chunk=chunk, tile=tile, first_tile_has_no_past=True)
    cols = lambda n: slice(n * MXU_TILE, (n + 1) * MXU_TILE)

    n_units, n_chunks, n_out = D_FF // MXU_TILE, tile // chunk, D_MODEL // MXU_TILE

    def gate_up(n):
        lo = n * MXU_TILE
        gate = _dot(xn_ref[...], wgu_ref[:, lo:lo + MXU_TILE])
        up = _dot(xn_ref[...], wgu_ref[:, D_FF + lo:D_FF + lo + MXU_TILE])
        act_ref[:, cols(n)] = (gate * jax.nn.sigmoid(gate) * up).astype(BF16)

    def down(n):
        x2_ref[:, cols(n)] = x2_ref[:, cols(n)] + 0.5 * _dot(act_ref[...], wd_ref[:, cols(n)])

    for n in range(n_out):
        x2_ref[:, cols(n)] = x1_ref[:, cols(n)] + _dot(mix_ref[...], wout_ref[:, cols(n)])
    stage()
    xn_ref[...] = _rms(x2_ref[...], g2_ref[...]).astype(BF16)
    scores, values = attend
    units = ([functools.partial(gate_up, n) for n in range(n_units)]
             + [functools.partial(down, n) for n in range(n_out)])
    prob = None
    for n, unit in enumerate(units):
        unit()
        if prob is not None:
            values(n - 1, prob)
        prob = scores(n) if n < n_chunks else None
    assert n_chunks < len(units)
    y_ref[...] = _rms(x2_ref[...], gfin_ref[...])
    shift, conv_block, n_blocks = conv
    shift()
    for r in range(n_blocks):
        conv_block(r)


def _mixer_parts(i, q_ref, kc_ref, kp_ref, vc_ref, vp_ref, uc_ref, up_ref, bias_ref, sink_ref,
                 cw_ref, cb_ref, lng_ref, lnb_ref, mix_ref, kw_ref, vw_ref, ue_ref, us_ref, ys_ref,
                 *, chunk, tile, first_tile_has_no_past):
    kwin_len = WINDOW + chunk
    n_bias = bias_ref.shape[0]

    def scores(c):
        low_half = lax.broadcasted_iota(jnp.int32, (chunk, LANES), 1) < HEAD_DIM
        sink = sink_ref[...]
        qc = q_ref[c * chunk:(c + 1) * chunk, :]
        zero = jnp.zeros_like(qc[:, :LANES])
        blocks = []
        for j in range(N_KV_HEADS):
            keep = low_half if j == 0 else jnp.logical_not(low_half)
            for p in range(GROUP):
                blocks.append(jnp.where(keep, qc[:, p * LANES:(p + 1) * LANES], zero))
        q_big = jnp.concatenate(blocks, axis=0)
        kwin = kw_ref[c * chunk:c * chunk + kwin_len, :]
        s = lax.dot_general(kwin, q_big, (((1,), (1,)), ((), ())), preferred_element_type=F32)
        bidx = jnp.minimum(i * (tile // chunk) + c, n_bias - 1)
        s = s + bias_ref[bidx]
        m = jnp.maximum(jnp.max(s, axis=0, keepdims=True), sink)
        e = jnp.exp(s - m)
        denom = jnp.sum(e, axis=0, keepdims=True) + jnp.exp(sink - m)
        return (e * (1.0 / denom)).astype(BF16)

    def values(c, prob):
        low_half = lax.broadcasted_iota(jnp.int32, (chunk, LANES), 1) < HEAD_DIM
        vwin = vw_ref[c * chunk:c * chunk + kwin_len, :]
        o = lax.dot_general(prob, vwin, (((0,), (0,)), ((), ())), preferred_element_type=F32)
        for p in range(GROUP):
            o0 = o[p * chunk:(p + 1) * chunk, :]
            o1 = o[(GROUP + p) * chunk:(GROUP + p + 1) * chunk, :]
            mix_ref[c * chunk:(c + 1) * chunk, p * LANES:(p + 1) * LANES] = (
                jnp.where(low_half, o0, o1).astype(BF16))

    def stage():
        kw_ref[0:WINDOW, :] = kp_ref[...].astype(BF16)
        kw_ref[WINDOW:, :] = kc_ref[...].astype(BF16)
        vw_ref[0:WINDOW, :] = vp_ref[...].astype(BF16)
        vw_ref[WINDOW:, :] = vc_ref[...].astype(BF16)
        u_prev = up_ref[...]
        if first_tile_has_no_past:
            u_prev = jnp.where(i > 0, u_prev, 0.0)
        past_rows = CONV_GROUPS * CONV_PAD
        ue_ref[0:past_rows, :] = u_prev
        ue_ref[past_rows:, :] = uc_ref[...]

    def shift():
        us_ref[...] = ue_ref[CONV_GROUPS:CONV_GROUPS + us_ref.shape[0], :]

    rows = min(CONV_ROWS, tile)
    block_rows = CONV_GROUPS * rows
    first_tap = CONV_PAD - (CONV_K - 1)

    def conv_block(r):
        acc = jnp.zeros((block_rows, LANES), F32)
        for j in range(CONV_K):
            lo = r * block_rows + CONV_GROUPS * (first_tap + j)
            if lo % SUBLANES == 0:
                taps = ue_ref[lo:lo + block_rows, :]
            else:
                taps = us_ref[lo - CONV_GROUPS:lo - CONV_GROUPS + block_rows, :]
            w = cw_ref[j * SUBLANES:(j + 1) * SUBLANES, :]
            acc = acc + taps * jnp.concatenate([w] * (block_rows // SUBLANES), axis=0)
        ys_ref[...] = acc + jnp.concatenate([cb_ref[...]] * (block_rows // SUBLANES), axis=0)
        y = jnp.concatenate([ys_ref[pl.ds(grp, rows, stride=CONV_GROUPS), :]
                             for grp in range(CONV_GROUPS)], axis=1)
        mu = jnp.mean(y, axis=-1, keepdims=True)
        yc = y - mu
        var = jnp.mean(yc * yc, axis=-1, keepdims=True)
        yn = yc * lax.rsqrt(var + EPS) * lng_ref[...] + lnb_ref[...]
        mix_ref[r * rows:(r + 1) * rows, ATT_WIDTH:] = (yn * jax.nn.sigmoid(yn)).astype(BF16)

    return stage, (scores, values), (shift, conv_block, tile // rows)


def _const_spec(shape):
    return pl.BlockSpec(shape, lambda *_: (0,) * len(shape), pipeline_mode=pl.Buffered(1))


def _params(n_axes):
    return pltpu.CompilerParams(dimension_semantics=("arbitrary",) * n_axes,
                                vmem_limit_bytes=VMEM_LIMIT)


def _ffn1_proj(x, g1, wgu, wd, gmix, win, gsum, gqk):
    n = x.shape[0]
    tm = min(TOKEN_TILE, n)
    row = lambda w: pl.BlockSpec((tm, w), lambda i: (i, 0))
    return pl.pallas_call(
        _ffn1_proj_kernel,
        grid=(n // tm,),
        in_specs=[row(D_MODEL), _const_spec(g1.shape), _const_spec(wgu.shape), _const_spec(wd.shape),
                  _const_spec(gmix.shape), _const_spec(win.shape), _const_spec(gsum.shape),
                  _const_spec(gqk.shape)],
        out_specs=[row(D_MODEL), row(ATT_WIDTH), row(KV_WIDTH), row(KV_WIDTH),
                   pl.BlockSpec((CONV_GROUPS * tm, LANES), lambda i: (i, 0))],
        out_shape=[jax.ShapeDtypeStruct((n, D_MODEL), F32), jax.ShapeDtypeStruct((n, ATT_WIDTH), BF16),
                   jax.ShapeDtypeStruct((n, KV_WIDTH), F32), jax.ShapeDtypeStruct((n, KV_WIDTH), F32),
                   jax.ShapeDtypeStruct((CONV_GROUPS * n, LANES), F32)],
        compiler_params=_params(1),
        name="ffn1_proj",
    )(x, g1, wgu, wd, gmix, win, gsum, gqk)


def _out_ffn2(x1, mix, wout, g2, wgu, wd, gfin):
    n = x1.shape[0]
    tm = min(TOKEN_TILE, n)
    row = lambda w: pl.BlockSpec((tm, w), lambda i: (i, 0))
    return pl.pallas_call(
        _out_ffn2_kernel,
        grid=(n // tm,),
        in_specs=[row(D_MODEL), row(D_MODEL), _const_spec(wout.shape), _const_spec(g2.shape),
                  _const_spec(wgu.shape), _const_spec(wd.shape), _const_spec(gfin.shape)],
        out_specs=row(D_MODEL),
        out_shape=jax.ShapeDtypeStruct((n, D_MODEL), F32),
        compiler_params=_params(1),
        name="out_ffn2",
    )(x1, mix, wout, g2, wgu, wd, gfin)


def _mixer_scratch(tile):
    return [pltpu.VMEM((WINDOW + tile, KV_WIDTH), BF16),
            pltpu.VMEM((WINDOW + tile, KV_WIDTH), BF16),
            pltpu.VMEM((CONV_GROUPS * (CONV_PAD + tile), LANES), F32),
            pltpu.VMEM((CONV_GROUPS * (CONV_PAD + tile) - SUBLANES, LANES), F32),
            pltpu.VMEM((CONV_GROUPS * min(CONV_ROWS, tile), LANES), F32)]


def _mixer(q, k, v, u, k_past, v_past, u_past, bias, sink_row, cw, cb, lng, lnb,
           *, n_streams, seq, chunk, tile):
    nt = seq // tile
    cur = lambda w: pl.BlockSpec((tile, w), lambda b, i: (b * nt + i, 0))
    kv_past = pl.BlockSpec((WINDOW, KV_WIDTH), lambda b, i: (b, 0))
    u_past_spec = pl.BlockSpec((CONV_GROUPS * CONV_PAD, LANES), lambda b, i: (b, 0))
    body = functools.partial(_mixer_kernel, chunk=chunk, tile=tile)
    return pl.pallas_call(
        body,
        grid=(n_streams, nt),
        in_specs=[cur(ATT_WIDTH), cur(KV_WIDTH), kv_past, cur(KV_WIDTH), kv_past,
                  pl.BlockSpec((CONV_GROUPS * tile, LANES), lambda b, i: (b * nt + i, 0)),
                  u_past_spec, _const_spec(bias.shape), _const_spec(sink_row.shape),
                  _const_spec(cw.shape), _const_spec(cb.shape), _const_spec(lng.shape),
                  _const_spec(lnb.shape)],
        out_specs=cur(D_MODEL),
        out_shape=jax.ShapeDtypeStruct((n_streams * seq, D_MODEL), BF16),
        scratch_shapes=_mixer_scratch(tile),
        compiler_params=_params(2),
        name="mixer",
    )(q, k, k_past, v, v_past, u, u_past, bias, sink_row, cw, cb, lng, lnb)


def _mixer_ffn2(q, k, v, u, bias, sink_row, cw, cb, lng, lnb, x1, wout, g2, wgu, wd, gfin,
                *, seq, chunk, tile):
    n_tiles = q.shape[0] // tile
    mix_tile = lambda s: jnp.minimum(s, n_tiles - 1)
    ffn_tile = lambda s: jnp.maximum(s - 1, 0)
    cur = lambda w: pl.BlockSpec((tile, w), lambda s: (mix_tile(s), 0))
    kv_past = pl.BlockSpec(
        (WINDOW, KV_WIDTH), lambda s: (jnp.maximum(mix_tile(s) * (tile // WINDOW) - 1, 0), 0))
    u_past = pl.BlockSpec(
        (CONV_GROUPS * CONV_PAD, LANES),
        lambda s: (jnp.maximum(mix_tile(s) * (tile // CONV_PAD) - 1, 0), 0))
    ffn_row = pl.BlockSpec((tile, D_MODEL), lambda s: (ffn_tile(s), 0))
    body = functools.partial(_mixer_ffn2_kernel, chunk=chunk, tile=tile, tiles_per_stream=seq // tile)
    return pl.pallas_call(
        body,
        grid=(n_tiles + 1,),
        in_specs=[cur(ATT_WIDTH), cur(KV_WIDTH), kv_past, cur(KV_WIDTH), kv_past,
                  pl.BlockSpec((CONV_GROUPS * tile, LANES), lambda s: (mix_tile(s), 0)),
                  u_past, _const_spec(bias.shape), _const_spec(sink_row.shape),
                  _const_spec(cw.shape), _const_spec(cb.shape), _const_spec(lng.shape),
                  _const_spec(lnb.shape),
                  ffn_row, _const_spec(wout.shape), _const_spec(g2.shape), _const_spec(wgu.shape),
                  _const_spec(wd.shape), _const_spec(gfin.shape)],
        out_specs=ffn_row,
        out_shape=jax.ShapeDtypeStruct((q.shape[0], D_MODEL), F32),
        scratch_shapes=[pltpu.VMEM((tile, D_MODEL), BF16), pltpu.VMEM((tile, D_MODEL), F32),
                        pltpu.VMEM((tile, D_MODEL), BF16), pltpu.VMEM((tile, D_FF), BF16)]
        + _mixer_scratch(tile),
        compiler_params=_params(1),
        name="mixer_ffn2",
    )(q, k, k, v, v, u, u, bias, sink_row, cw, cb, lng, lnb, x1, wout, g2, wgu, wd, gfin)


def _rel_bucket(rel):
    nb = NUM_BUCKETS // 2
    max_exact = nb // 2
    ret = np.where(rel > 0, nb, 0)
    n = np.abs(rel)
    nf = np.maximum(n, 1).astype(np.float32)
    large = max_exact + (np.log(nf / np.float32(max_exact)) / np.float32(math.log(MAX_DISTANCE / max_exact))
                         * np.float32(nb - max_exact)).astype(np.int32)
    large = np.minimum(large, nb - 1)
    return (ret + np.where(n < max_exact, n, large)).astype(np.int32)


def _bias_cols(table, qpos, kpos):
    bucket = _rel_bucket(kpos[:, None] - qpos[None, :])
    onehot = jnp.asarray(bucket[:, :, None] == np.arange(NUM_BUCKETS), dtype=F32)
    b = jnp.einsum("kqb,bh->khq", onehot, table.astype(F32), precision=lax.Precision.HIGHEST)
    return b.reshape(kpos.shape[0], N_HEADS * qpos.shape[0])


def kernel(x_prompt, x_sample, cache_k, cache_v, state_conv, rel_bias_table, ffn1_norm, ffn1_w_gu,
           ffn1_w_down, mix_norm, w_in, q_norm, k_norm, sinks, conv_w, conv_b, conv_ln_g, conv_ln_b,
           w_out, ffn2_norm, ffn2_w_gu, ffn2_w_down, final_norm):
    B, S, _ = x_prompt.shape
    DB, DS, _ = x_sample.shape
    R = cache_k.shape[2]
    past_len = R
    assert ffn1_norm.shape[0] == 1 and R == WINDOW and S % MIX_TILE == 0 and DS % 8 == 0
    l = 0
    row = lambda a: a.reshape(1, -1).astype(F32)

    pair = lambda a, axis: jnp.swapaxes(
        a.reshape(a.shape[:axis] + (N_KV_HEADS, GROUP, HEAD_DIM) + a.shape[axis + 1:]),
        axis, axis + 1).reshape(a.shape)
    win = jnp.concatenate([pair(w_in[l][:, :ATT_WIDTH], 1), w_in[l][:, ATT_WIDTH:]], axis=1).astype(BF16)
    wout = jnp.concatenate([pair(w_out[l][:ATT_WIDTH], 0), w_out[l][ATT_WIDTH:]], axis=0).astype(BF16)
    gqk = jnp.concatenate([jnp.tile(q_norm[l], N_HEADS) * (HEAD_DIM ** -0.5),
                           jnp.tile(k_norm[l], N_KV_HEADS)]).reshape(1, QK_WIDTH).astype(F32)
    head_of_col = np.arange(NORM_GROUP) // HEAD_DIM
    gsum = jnp.asarray(head_of_col[:, None] == head_of_col[None, :], dtype=BF16)
    wgu1, wd1 = ffn1_w_gu[l].astype(BF16), ffn1_w_down[l].astype(BF16)
    wgu2, wd2 = ffn2_w_gu[l].astype(BF16), ffn2_w_down[l].astype(BF16)
    by_group = lambda a: jnp.tile(a.astype(F32).reshape(-1, CONV_GROUPS, LANES),
                                  (1, SUBLANES // CONV_GROUPS, 1)).reshape(-1, LANES)
    cw, cb = by_group(conv_w[l]), by_group(conv_b[l])
    lng, lnb = row(conv_ln_g[l]), row(conv_ln_b[l])

    band = WINDOW + CHUNK
    bias_p = _bias_cols(rel_bias_table, np.arange(CHUNK), np.arange(band) - WINDOW)
    key = np.arange(band)[:, None]
    neg = lambda first_valid: jnp.where(key >= first_valid, bias_p, -jnp.inf)
    bias_prompt = jnp.stack([neg(WINDOW), neg(WINDOW - CHUNK), bias_p])
    qpos_s = past_len + np.arange(DS)
    kpos_s = np.concatenate([past_len - R + np.arange(R), qpos_s])
    bias_sample = _bias_cols(rel_bias_table, qpos_s, kpos_s)[None]
    sink_row = lambda n: jnp.broadcast_to(sinks[l].astype(F32)[:, None], (N_HEADS, n)).reshape(1, N_HEADS * n)

    def layer(x, n_streams, seq, chunk, tile, past):
        x1, q, k, v, u = _ffn1_proj(x, row(ffn1_norm[l]), wgu1, wd1, row(mix_norm[l]), win, gsum, gqk)
        if past is None:
            y = _mixer_ffn2(q, k, v, u, bias_prompt, sink_row(chunk), cw, cb, lng, lnb, x1, wout,
                            row(ffn2_norm[l]), wgu2, wd2, row(final_norm[l]),
                            seq=seq, chunk=chunk, tile=tile)
        else:
            mix = _mixer(q, k, v, u, *past, bias_sample, sink_row(chunk), cw, cb, lng, lnb,
                         n_streams=n_streams, seq=seq, chunk=chunk, tile=tile)
            y = _out_ffn2(x1, mix, wout, row(ffn2_norm[l]), wgu2, wd2, row(final_norm[l]))
        return y, k, v, u

    yp, kp, vp, up = layer(x_prompt.reshape(B * S, D_MODEL), B, S, CHUNK, MIX_TILE, None)
    ck = cache_k[l].reshape(DB * R, KV_WIDTH).astype(F32)
    cv = cache_v[l].reshape(DB * R, KV_WIDTH).astype(F32)
    sc = jnp.pad(state_conv[l].astype(F32), ((0, 0), (CONV_PAD - (CONV_K - 1), 0), (0, 0)))
    ys, ks, vs, us = layer(x_sample.reshape(DB * DS, D_MODEL), DB, DS, DS, DS,
                           (ck, cv, sc.reshape(DB * CONV_PAD * CONV_GROUPS, LANES)))

    RP = min(WINDOW, S)
    heads = lambda a, n, t: a.reshape(n, t, N_KV_HEADS, HEAD_DIM)
    tail = lambda a, n: a.reshape(B, -1, a.shape[-1])[:, -n * (a.shape[0] // (B * S)):]
    new_k_prompt = heads(tail(kp, RP), B, RP)[None]
    new_v_prompt = heads(tail(vp, RP), B, RP)[None]
    new_conv_prompt = tail(up, CONV_K - 1).reshape(B, CONV_K - 1, CONV_WIDTH)[None]
    new_k_sample = jnp.concatenate([cache_k[l].astype(F32), heads(ks, DB, DS)], axis=1)[:, DS:][None]
    new_v_sample = jnp.concatenate([cache_v[l].astype(F32), heads(vs, DB, DS)], axis=1)[:, DS:][None]
    new_conv_sample = jnp.concatenate([state_conv[l].astype(F32), us.reshape(DB, DS, CONV_WIDTH)],
                                      axis=1)[:, DS:][None]
    return (yp.reshape(B, S, D_MODEL), ys.reshape(DB, DS, D_MODEL), new_k_prompt, new_v_prompt,
            new_conv_prompt, new_k_sample, new_v_sample, new_conv_sample)
```

```python
import functools
import math

import numpy as np
import jax
import jax.numpy as jnp
from jax import lax
from jax.experimental import pallas as pl
from jax.experimental.pallas import tpu as pltpu

D_MODEL = 1024
CHUNK = 64
HEAD_DIM = 64
N_HEADS = 8
N_KV_HEADS = 2
GROUP = N_HEADS // N_KV_HEADS
ATT_WIDTH = N_HEADS * HEAD_DIM
KV_WIDTH = N_KV_HEADS * HEAD_DIM
CONV_WIDTH = D_MODEL - ATT_WIDTH
CONV_K = 31
WINDOW = 128
IN_WIDTH = ATT_WIDTH + 2 * KV_WIDTH + 2 * CONV_WIDTH
QK_WIDTH = ATT_WIDTH + KV_WIDTH
D_FF = 2816
NUM_BUCKETS = 32
MAX_DISTANCE = 128
EPS = 1e-6

LANES = 128
CONV_PAD = 32
FF_SPLITS = (0, 1536, D_FF)
SUBLANES = 8
MXU_TILE = 256
NORM_GROUP = MXU_TILE
CONV_ROWS = 32
CONV_GROUPS = CONV_WIDTH // LANES
TOKEN_TILE = 1024
MIX_TILE = 512
VMEM_LIMIT = 56 * 1024 * 1024

F32 = jnp.float32
BF16 = jnp.bfloat16


def _dot(a, b):
    return jnp.dot(a, b, preferred_element_type=F32)


def _rms(x, g):
    ms = jnp.mean(x * x, axis=-1, keepdims=True)
    return x * lax.rsqrt(ms + EPS) * g


def _swiglu(xn, wgu_ref, wd_ref):
    acc = None
    for lo, hi in zip(FF_SPLITS[:-1], FF_SPLITS[1:]):
        gate = _dot(xn, wgu_ref[:, lo:hi])
        up = _dot(xn, wgu_ref[:, D_FF + lo:D_FF + hi])
        act = (gate * jax.nn.sigmoid(gate) * up).astype(BF16)
        part = _dot(act, wd_ref[lo:hi, :])
        acc = part if acc is None else acc + part
    return acc


def _ffn1_proj_kernel(x_ref, g1_ref, wgu_ref, wd_ref, gmix_ref, win_ref, gsum_ref, gqk_ref,
                      x1_ref, q_ref, k_ref, v_ref, u_ref):
    x = x_ref[...]
    x1 = x + 0.5 * _swiglu(_rms(x, g1_ref[...]).astype(BF16), wgu_ref, wd_ref)
    x1_ref[...] = x1
    z = _dot(_rms(x1, gmix_ref[...]).astype(BF16), win_ref[...])
    qk = z[:, :QK_WIDTH]
    sq = (qk * qk).astype(BF16)
    ss = jnp.concatenate(
        [_dot(sq[:, lo:lo + NORM_GROUP], gsum_ref[...]) for lo in range(0, ATT_WIDTH, NORM_GROUP)]
        + [_dot(sq[:, ATT_WIDTH:], gsum_ref[:KV_WIDTH, :KV_WIDTH])], axis=1)
    qkn = qk * lax.rsqrt(ss * (1.0 / HEAD_DIM) + EPS) * gqk_ref[...]
    q_ref[...] = qkn[:, :ATT_WIDTH].astype(BF16)
    k_ref[...] = qkn[:, ATT_WIDTH:]
    v_ref[...] = z[:, QK_WIDTH:QK_WIDTH + KV_WIDTH]
    a = z[:, QK_WIDTH + KV_WIDTH:QK_WIDTH + KV_WIDTH + CONV_WIDTH]
    g = z[:, QK_WIDTH + KV_WIDTH + CONV_WIDTH:]
    u = a * jax.nn.sigmoid(g)
    for grp in range(CONV_GROUPS):
        u_ref[pl.ds(grp, u.shape[0], stride=CONV_GROUPS), :] = u[:, grp * LANES:(grp + 1) * LANES]


def _out_ffn2_kernel(x1_ref, mix_ref, wout_ref, g2_ref, wgu_ref, wd_ref, gfin_ref, y_ref):
    x2 = x1_ref[...] + _dot(mix_ref[...], wout_ref[...])
    x3 = x2 + 0.5 * _swiglu(_rms(x2, g2_ref[...]).astype(BF16), wgu_ref, wd_ref)
    y_ref[...] = _rms(x3, gfin_ref[...])


def _zero_of(x):
    bits = pltpu.bitcast(x, jnp.uint32)
    half = jnp.uint32(16)
    return lax.shift_right_logical(lax.shift_right_logical(bits, half), half)


def _after(x, zeros):
    packed = 2 * SUBLANES
    top = pltpu.bitcast(x[0:packed, :], jnp.uint32)
    top = top | jnp.concatenate([zeros] * (x.shape[1] // LANES), axis=1)
    return jnp.concatenate([pltpu.bitcast(top, x.dtype), x[packed:, :]], axis=0)


def _mixer_kernel(*refs, chunk, tile):
    stage, attend, conv = _mixer_parts(pl.program_id(1), *refs, chunk=chunk, tile=tile,
                                       first_tile_has_no_past=False)
    stage()
    scores, values = attend
    for c in range(tile // chunk):
        values(c, scores(c)[0])
    shift, conv_block, n_blocks = conv
    shift()
    for r in range(n_blocks):
        conv_block(r)


def _mixer_ffn2_kernel(q_ref, kc_ref, kp_ref, vc_ref, vp_ref, uc_ref, up_ref, bias_ref, sink_ref,
                       cw_ref, cb_ref, lng_ref, lnb_ref,
                       x1_ref, wout_ref, g2_ref, wgu_ref, wd_ref, gfin_ref,
                       y_ref, mix_ref, x2_ref, xn_ref, act_ref, kw_ref, vw_ref, ue_ref, us_ref, ys_ref,
                       *, chunk, tile, tiles_per_stream):
    s = pl.program_id(0)

    @pl.when(s == 0)
    def _():
        mix_ref[...] = jnp.zeros_like(mix_ref)

    t = jnp.minimum(s, pl.num_programs(0) - 2)
    stage, attend, conv = _mixer_parts(
        lax.rem(t, tiles_per_stream), q_ref, kc_ref, kp_ref, vc_ref, vp_ref, uc_ref, up_ref, bias_ref,
        sink_ref, cw_ref, cb_ref, lng_ref, lnb_ref, mix_ref, kw_ref, vw_ref, ue_ref, us_ref, ys_ref,
        chunk=chunk, tile=tile, first_tile_has_no_past=True)
    cols = lambda n: slice(n * MXU_TILE, (n + 1) * MXU_TILE)

    n_units, n_chunks, n_out = D_FF // MXU_TILE, tile // chunk, D_MODEL // MXU_TILE

    def gate_up(n):
        lo = n * MXU_TILE
        gate = _dot(xn_ref[...], wgu_ref[:, lo:lo + MXU_TILE])
        up = _dot(xn_ref[...], wgu_ref[:, D_FF + lo:D_FF + lo + MXU_TILE])
        act_ref[:, cols(n)] = (gate * jax.nn.sigmoid(gate) * up).astype(BF16)

    def down(n):
        x2_ref[:, cols(n)] = x2_ref[:, cols(n)] + 0.5 * _dot(act_ref[...], wd_ref[:, cols(n)])

    for n in range(n_out):
        x2_ref[:, cols(n)] = x1_ref[:, cols(n)] + _dot(mix_ref[...], wout_ref[:, cols(n)])
    stage()
    xn_ref[...] = _rms(x2_ref[...], g2_ref[...]).astype(BF16)
    scores, values = attend
    units = ([functools.partial(gate_up, n) for n in range(n_units)]
             + [functools.partial(down, n) for n in range(n_out)])
    shift, conv_block, n_blocks = conv
    shift()
    prob = None
    blocks = iter(range(n_blocks))
    for n, unit in enumerate(units):
        unit()
        if prob is not None:
            values(n - 1, prob)
        prob = None
        if n < n_chunks:
            prob, started = scores(n)
            r = next(blocks, None)
            if r is not None:
                prob = _after(prob, conv_block(r, started))
    assert n_chunks < len(units)
    y_ref[...] = _rms(x2_ref[...], gfin_ref[...])
    for r in blocks:
        conv_block(r)


def _mixer_parts(i, q_ref, kc_ref, kp_ref, vc_ref, vp_ref, uc_ref, up_ref, bias_ref, sink_ref,
                 cw_ref, cb_ref, lng_ref, lnb_ref, mix_ref, kw_ref, vw_ref, ue_ref, us_ref, ys_ref,
                 *, chunk, tile, first_tile_has_no_past):
    kwin_len = WINDOW + chunk
    n_bias = bias_ref.shape[0]

    def scores(c):
        low_half = lax.broadcasted_iota(jnp.int32, (chunk, LANES), 1) < HEAD_DIM
        sink = sink_ref[...]
        qc = q_ref[c * chunk:(c + 1) * chunk, :]
        zero = jnp.zeros_like(qc[:, :LANES])
        blocks = []
        for j in range(N_KV_HEADS):
            keep = low_half if j == 0 else jnp.logical_not(low_half)
            for p in range(GROUP):
                blocks.append(jnp.where(keep, qc[:, p * LANES:(p + 1) * LANES], zero))
        q_big = jnp.concatenate(blocks, axis=0)
        kwin = kw_ref[c * chunk:c * chunk + kwin_len, :]
        s = lax.dot_general(kwin, q_big, (((1,), (1,)), ((), ())), preferred_element_type=F32)
        bidx = jnp.minimum(i * (tile // chunk) + c, n_bias - 1)
        sb = s + bias_ref[bidx]
        m = jnp.maximum(jnp.max(sb, axis=0, keepdims=True), sink)
        e = jnp.exp(sb - m)
        denom = jnp.sum(e, axis=0, keepdims=True) + jnp.exp(sink - m)
        return (e * (1.0 / denom)).astype(BF16), _zero_of(s[0:SUBLANES, 0:LANES])

    def values(c, prob):
        low_half = lax.broadcasted_iota(jnp.int32, (chunk, LANES), 1) < HEAD_DIM
        vwin = vw_ref[c * chunk:c * chunk + kwin_len, :]
        o = lax.dot_general(prob, vwin, (((0,), (0,)), ((), ())), preferred_element_type=F32)
        for p in range(GROUP):
            o0 = o[p * chunk:(p + 1) * chunk, :]
            o1 = o[(GROUP + p) * chunk:(GROUP + p + 1) * chunk, :]
            mix_ref[c * chunk:(c + 1) * chunk, p * LANES:(p + 1) * LANES] = (
                jnp.where(low_half, o0, o1).astype(BF16))

    def stage():
        kw_ref[0:WINDOW, :] = kp_ref[...].astype(BF16)
        kw_ref[WINDOW:, :] = kc_ref[...].astype(BF16)
        vw_ref[0:WINDOW, :] = vp_ref[...].astype(BF16)
        vw_ref[WINDOW:, :] = vc_ref[...].astype(BF16)
        u_prev = up_ref[...]
        if first_tile_has_no_past:
            u_prev = jnp.where(i > 0, u_prev, 0.0)
        past_rows = CONV_GROUPS * CONV_PAD
        ue_ref[0:past_rows, :] = u_prev
        ue_ref[past_rows:, :] = uc_ref[...]

    def shift():
        us_ref[...] = ue_ref[CONV_GROUPS:CONV_GROUPS + us_ref.shape[0], :]

    rows = min(CONV_ROWS, tile)
    block_rows = CONV_GROUPS * rows
    first_tap = CONV_PAD - (CONV_K - 1)

    def conv_block(r, start=None):
        if start is None:
            acc = jnp.zeros((block_rows, LANES), F32)
        else:
            acc = pltpu.bitcast(jnp.concatenate([start] * (block_rows // SUBLANES), axis=0), F32)
        for j in range(CONV_K):
            lo = r * block_rows + CONV_GROUPS * (first_tap + j)
            if lo % SUBLANES == 0:
                taps = ue_ref[lo:lo + block_rows, :]
            else:
                taps = us_ref[lo - CONV_GROUPS:lo - CONV_GROUPS + block_rows, :]
            w = cw_ref[j * SUBLANES:(j + 1) * SUBLANES, :]
            acc = acc + taps * jnp.concatenate([w] * (block_rows // SUBLANES), axis=0)
        ys_ref[...] = acc + jnp.concatenate([cb_ref[...]] * (block_rows // SUBLANES), axis=0)
        y = jnp.concatenate([ys_ref[pl.ds(grp, rows, stride=CONV_GROUPS), :]
                             for grp in range(CONV_GROUPS)], axis=1)
        mu = jnp.mean(y, axis=-1, keepdims=True)
        yc = y - mu
        var = jnp.mean(yc * yc, axis=-1, keepdims=True)
        yn = yc * lax.rsqrt(var + EPS) * lng_ref[...] + lnb_ref[...]
        out = (yn * jax.nn.sigmoid(yn)).astype(BF16)
        mix_ref[r * rows:(r + 1) * rows, ATT_WIDTH:] = out
        packed = 2 * SUBLANES
        zeros = _zero_of(out[0:packed, 0:LANES])
        for lo in range(packed, rows, packed):
            zeros = zeros | _zero_of(out[lo:lo + packed, 0:LANES])
        return zeros

    return stage, (scores, values), (shift, conv_block, tile // rows)


def _const_spec(shape):
    return pl.BlockSpec(shape, lambda *_: (0,) * len(shape), pipeline_mode=pl.Buffered(1))


def _params(n_axes):
    return pltpu.CompilerParams(dimension_semantics=("arbitrary",) * n_axes,
                                vmem_limit_bytes=VMEM_LIMIT)


def _ffn1_proj(x, g1, wgu, wd, gmix, win, gsum, gqk):
    n = x.shape[0]
    tm = min(TOKEN_TILE, n)
    row = lambda w: pl.BlockSpec((tm, w), lambda i: (i, 0))
    return pl.pallas_call(
        _ffn1_proj_kernel,
        grid=(n // tm,),
        in_specs=[row(D_MODEL), _const_spec(g1.shape), _const_spec(wgu.shape), _const_spec(wd.shape),
                  _const_spec(gmix.shape), _const_spec(win.shape), _const_spec(gsum.shape),
                  _const_spec(gqk.shape)],
        out_specs=[row(D_MODEL), row(ATT_WIDTH), row(KV_WIDTH), row(KV_WIDTH),
                   pl.BlockSpec((CONV_GROUPS * tm, LANES), lambda i: (i, 0))],
        out_shape=[jax.ShapeDtypeStruct((n, D_MODEL), F32), jax.ShapeDtypeStruct((n, ATT_WIDTH), BF16),
                   jax.ShapeDtypeStruct((n, KV_WIDTH), F32), jax.ShapeDtypeStruct((n, KV_WIDTH), F32),
                   jax.ShapeDtypeStruct((CONV_GROUPS * n, LANES), F32)],
        compiler_params=_params(1),
        name="ffn1_proj",
    )(x, g1, wgu, wd, gmix, win, gsum, gqk)


def _out_ffn2(x1, mix, wout, g2, wgu, wd, gfin):
    n = x1.shape[0]
    tm = min(TOKEN_TILE, n)
    row = lambda w: pl.BlockSpec((tm, w), lambda i: (i, 0))
    return pl.pallas_call(
        _out_ffn2_kernel,
        grid=(n // tm,),
        in_specs=[row(D_MODEL), row(D_MODEL), _const_spec(wout.shape), _const_spec(g2.shape),
                  _const_spec(wgu.shape), _const_spec(wd.shape), _const_spec(gfin.shape)],
        out_specs=row(D_MODEL),
        out_shape=jax.ShapeDtypeStruct((n, D_MODEL), F32),
        compiler_params=_params(1),
        name="out_ffn2",
    )(x1, mix, wout, g2, wgu, wd, gfin)


def _mixer_scratch(tile):
    return [pltpu.VMEM((WINDOW + tile, KV_WIDTH), BF16),
            pltpu.VMEM((WINDOW + tile, KV_WIDTH), BF16),
            pltpu.VMEM((CONV_GROUPS * (CONV_PAD + tile), LANES), F32),
            pltpu.VMEM((CONV_GROUPS * (CONV_PAD + tile) - SUBLANES, LANES), F32),
            pltpu.VMEM((CONV_GROUPS * min(CONV_ROWS, tile), LANES), F32)]


def _mixer(q, k, v, u, k_past, v_past, u_past, bias, sink_row, cw, cb, lng, lnb,
           *, n_streams, seq, chunk, tile):
    nt = seq // tile
    cur = lambda w: pl.BlockSpec((tile, w), lambda b, i: (b * nt + i, 0))
    kv_past = pl.BlockSpec((WINDOW, KV_WIDTH), lambda b, i: (b, 0))
    u_past_spec = pl.BlockSpec((CONV_GROUPS * CONV_PAD, LANES), lambda b, i: (b, 0))
    body = functools.partial(_mixer_kernel, chunk=chunk, tile=tile)
    return pl.pallas_call(
        body,
        grid=(n_streams, nt),
        in_specs=[cur(ATT_WIDTH), cur(KV_WIDTH), kv_past, cur(KV_WIDTH), kv_past,
                  pl.BlockSpec((CONV_GROUPS * tile, LANES), lambda b, i: (b * nt + i, 0)),
                  u_past_spec, _const_spec(bias.shape), _const_spec(sink_row.shape),
                  _const_spec(cw.shape), _const_spec(cb.shape), _const_spec(lng.shape),
                  _const_spec(lnb.shape)],
        out_specs=cur(D_MODEL),
        out_shape=jax.ShapeDtypeStruct((n_streams * seq, D_MODEL), BF16),
        scratch_shapes=_mixer_scratch(tile),
        compiler_params=_params(2),
        name="mixer",
    )(q, k, k_past, v, v_past, u, u_past, bias, sink_row, cw, cb, lng, lnb)


def _mixer_ffn2(q, k, v, u, bias, sink_row, cw, cb, lng, lnb, x1, wout, g2, wgu, wd, gfin,
                *, seq, chunk, tile):
    n_tiles = q.shape[0] // tile
    mix_tile = lambda s: jnp.minimum(s, n_tiles - 1)
    ffn_tile = lambda s: jnp.maximum(s - 1, 0)
    cur = lambda w: pl.BlockSpec((tile, w), lambda s: (mix_tile(s), 0))
    kv_past = pl.BlockSpec(
        (WINDOW, KV_WIDTH), lambda s: (jnp.maximum(mix_tile(s) * (tile // WINDOW) - 1, 0), 0))
    u_past = pl.BlockSpec(
        (CONV_GROUPS * CONV_PAD, LANES),
        lambda s: (jnp.maximum(mix_tile(s) * (tile // CONV_PAD) - 1, 0), 0))
    ffn_row = pl.BlockSpec((tile, D_MODEL), lambda s: (ffn_tile(s), 0))
    body = functools.partial(_mixer_ffn2_kernel, chunk=chunk, tile=tile, tiles_per_stream=seq // tile)
    return pl.pallas_call(
        body,
        grid=(n_tiles + 1,),
        in_specs=[cur(ATT_WIDTH), cur(KV_WIDTH), kv_past, cur(KV_WIDTH), kv_past,
                  pl.BlockSpec((CONV_GROUPS * tile, LANES), lambda s: (mix_tile(s), 0)),
                  u_past, _const_spec(bias.shape), _const_spec(sink_row.shape),
                  _const_spec(cw.shape), _const_spec(cb.shape), _const_spec(lng.shape),
                  _const_spec(lnb.shape),
                  ffn_row, _const_spec(wout.shape), _const_spec(g2.shape), _const_spec(wgu.shape),
                  _const_spec(wd.shape), _const_spec(gfin.shape)],
        out_specs=ffn_row,
        out_shape=jax.ShapeDtypeStruct((q.shape[0], D_MODEL), F32),
        scratch_shapes=[pltpu.VMEM((tile, D_MODEL), BF16), pltpu.VMEM((tile, D_MODEL), F32),
                        pltpu.VMEM((tile, D_MODEL), BF16), pltpu.VMEM((tile, D_FF), BF16)]
        + _mixer_scratch(tile),
        compiler_params=_params(1),
        name="mixer_ffn2",
    )(q, k, k, v, v, u, u, bias, sink_row, cw, cb, lng, lnb, x1, wout, g2, wgu, wd, gfin)


def _rel_bucket(rel):
    nb = NUM_BUCKETS // 2
    max_exact = nb // 2
    ret = np.where(rel > 0, nb, 0)
    n = np.abs(rel)
    nf = np.maximum(n, 1).astype(np.float32)
    large = max_exact + (np.log(nf / np.float32(max_exact)) / np.float32(math.log(MAX_DISTANCE / max_exact))
                         * np.float32(nb - max_exact)).astype(np.int32)
    large = np.minimum(large, nb - 1)
    return (ret + np.where(n < max_exact, n, large)).astype(np.int32)


def _bias_cols(table, qpos, kpos):
    bucket = _rel_bucket(kpos[:, None] - qpos[None, :])
    onehot = jnp.asarray(bucket[:, :, None] == np.arange(NUM_BUCKETS), dtype=F32)
    b = jnp.einsum("kqb,bh->khq", onehot, table.astype(F32), precision=lax.Precision.HIGHEST)
    return b.reshape(kpos.shape[0], N_HEADS * qpos.shape[0])


def kernel(x_prompt, x_sample, cache_k, cache_v, state_conv, rel_bias_table, ffn1_norm, ffn1_w_gu,
           ffn1_w_down, mix_norm, w_in, q_norm, k_norm, sinks, conv_w, conv_b, conv_ln_g, conv_ln_b,
           w_out, ffn2_norm, ffn2_w_gu, ffn2_w_down, final_norm):
    B, S, _ = x_prompt.shape
    DB, DS, _ = x_sample.shape
    R = cache_k.shape[2]
    past_len = R
    assert ffn1_norm.shape[0] == 1 and R == WINDOW and S % MIX_TILE == 0 and DS % 8 == 0
    l = 0
    row = lambda a: a.reshape(1, -1).astype(F32)

    pair = lambda a, axis: jnp.swapaxes(
        a.reshape(a.shape[:axis] + (N_KV_HEADS, GROUP, HEAD_DIM) + a.shape[axis + 1:]),
        axis, axis + 1).reshape(a.shape)
    win = jnp.concatenate([pair(w_in[l][:, :ATT_WIDTH], 1), w_in[l][:, ATT_WIDTH:]], axis=1).astype(BF16)
    wout = jnp.concatenate([pair(w_out[l][:ATT_WIDTH], 0), w_out[l][ATT_WIDTH:]], axis=0).astype(BF16)
    gqk = jnp.concatenate([jnp.tile(q_norm[l], N_HEADS) * (HEAD_DIM ** -0.5),
                           jnp.tile(k_norm[l], N_KV_HEADS)]).reshape(1, QK_WIDTH).astype(F32)
    head_of_col = np.arange(NORM_GROUP) // HEAD_DIM
    gsum = jnp.asarray(head_of_col[:, None] == head_of_col[None, :], dtype=BF16)
    wgu1, wd1 = ffn1_w_gu[l].astype(BF16), ffn1_w_down[l].astype(BF16)
    wgu2, wd2 = ffn2_w_gu[l].astype(BF16), ffn2_w_down[l].astype(BF16)
    by_group = lambda a: jnp.tile(a.astype(F32).reshape(-1, CONV_GROUPS, LANES),
                                  (1, SUBLANES // CONV_GROUPS, 1)).reshape(-1, LANES)
    cw, cb = by_group(conv_w[l]), by_group(conv_b[l])
    lng, lnb = row(conv_ln_g[l]), row(conv_ln_b[l])

    band = WINDOW + CHUNK
    bias_p = _bias_cols(rel_bias_table, np.arange(CHUNK), np.arange(band) - WINDOW)
    key = np.arange(band)[:, None]
    neg = lambda first_valid: jnp.where(key >= first_valid, bias_p, -jnp.inf)
    bias_prompt = jnp.stack([neg(WINDOW), neg(WINDOW - CHUNK), bias_p])
    qpos_s = past_len + np.arange(DS)
    kpos_s = np.concatenate([past_len - R + np.arange(R), qpos_s])
    bias_sample = _bias_cols(rel_bias_table, qpos_s, kpos_s)[None]
    sink_row = lambda n: jnp.broadcast_to(sinks[l].astype(F32)[:, None], (N_HEADS, n)).reshape(1, N_HEADS * n)

    def layer(x, n_streams, seq, chunk, tile, past):
        x1, q, k, v, u = _ffn1_proj(x, row(ffn1_norm[l]), wgu1, wd1, row(mix_norm[l]), win, gsum, gqk)
        if past is None:
            y = _mixer_ffn2(q, k, v, u, bias_prompt, sink_row(chunk), cw, cb, lng, lnb, x1, wout,
                            row(ffn2_norm[l]), wgu2, wd2, row(final_norm[l]),
                            seq=seq, chunk=chunk, tile=tile)
        else:
            mix = _mixer(q, k, v, u, *past, bias_sample, sink_row(chunk), cw, cb, lng, lnb,
                         n_streams=n_streams, seq=seq, chunk=chunk, tile=tile)
            y = _out_ffn2(x1, mix, wout, row(ffn2_norm[l]), wgu2, wd2, row(final_norm[l]))
        return y, k, v, u

    yp, kp, vp, up = layer(x_prompt.reshape(B * S, D_MODEL), B, S, CHUNK, MIX_TILE, None)
    ck = cache_k[l].reshape(DB * R, KV_WIDTH).astype(F32)
    cv = cache_v[l].reshape(DB * R, KV_WIDTH).astype(F32)
    sc = jnp.pad(state_conv[l].astype(F32), ((0, 0), (CONV_PAD - (CONV_K - 1), 0), (0, 0)))
    ys, ks, vs, us = layer(x_sample.reshape(DB * DS, D_MODEL), DB, DS, DS, DS,
                           (ck, cv, sc.reshape(DB * CONV_PAD * CONV_GROUPS, LANES)))

    RP = min(WINDOW, S)
    heads = lambda a, n, t: a.reshape(n, t, N_KV_HEADS, HEAD_DIM)
    tail = lambda a, n: a.reshape(B, -1, a.shape[-1])[:, -n * (a.shape[0] // (B * S)):]
    new_k_prompt = heads(tail(kp, RP), B, RP)[None]
    new_v_prompt = heads(tail(vp, RP), B, RP)[None]
    new_conv_prompt = tail(up, CONV_K - 1).reshape(B, CONV_K - 1, CONV_WIDTH)[None]
    new_k_sample = jnp.concatenate([cache_k[l].astype(F32), heads(ks, DB, DS)], axis=1)[:, DS:][None]
    new_v_sample = jnp.concatenate([cache_v[l].astype(F32), heads(vs, DB, DS)], axis=1)[:, DS:][None]
    new_conv_sample = jnp.concatenate([state_conv[l].astype(F32), us.reshape(DB, DS, CONV_WIDTH)],
                                      axis=1)[:, DS:][None]
    return (yp.reshape(B, S, D_MODEL), ys.reshape(DB, DS, D_MODEL), new_k_prompt, new_v_prompt,
            new_conv_prompt, new_k_sample, new_v_sample, new_conv_sample)
```

```python
import functools
import itertools
import math

import numpy as np
import jax
import jax.numpy as jnp
from jax import lax
from jax.experimental import pallas as pl
from jax.experimental.pallas import tpu as pltpu

D_MODEL = 1024
CHUNK = 64
HEAD_DIM = 64
N_HEADS = 8
N_KV_HEADS = 2
GROUP = N_HEADS // N_KV_HEADS
ATT_WIDTH = N_HEADS * HEAD_DIM
KV_WIDTH = N_KV_HEADS * HEAD_DIM
CONV_WIDTH = D_MODEL - ATT_WIDTH
CONV_K = 31
WINDOW = 128
IN_WIDTH = ATT_WIDTH + 2 * KV_WIDTH + 2 * CONV_WIDTH
QK_WIDTH = ATT_WIDTH + KV_WIDTH
D_FF = 2816
NUM_BUCKETS = 32
MAX_DISTANCE = 128
EPS = 1e-6

LANES = 128
CONV_PAD = 32
FF_SPLITS = (0, 1536, D_FF)
SUBLANES = 8
MXU_TILE = 256
NORM_GROUP = MXU_TILE
CONV_ROWS = 16
CONV_GROUPS = CONV_WIDTH // LANES
TOKEN_TILE = 1024
MIX_TILE = 512
ATT_PIECE = 64
BLOCKS_PER_PIECE = 1
VMEM_LIMIT = 56 * 1024 * 1024

F32 = jnp.float32
BF16 = jnp.bfloat16


def _dot(a, b):
    return jnp.dot(a, b, preferred_element_type=F32)


def _rms(x, g):
    ms = jnp.mean(x * x, axis=-1, keepdims=True)
    return x * lax.rsqrt(ms + EPS) * g


def _swiglu(xn, wgu_ref, wd_ref):
    acc = None
    for lo, hi in zip(FF_SPLITS[:-1], FF_SPLITS[1:]):
        gate = _dot(xn, wgu_ref[:, lo:hi])
        up = _dot(xn, wgu_ref[:, D_FF + lo:D_FF + hi])
        act = (gate * jax.nn.sigmoid(gate) * up).astype(BF16)
        part = _dot(act, wd_ref[lo:hi, :])
        acc = part if acc is None else acc + part
    return acc


def _ffn1_proj_kernel(x_ref, g1_ref, wgu_ref, wd_ref, gmix_ref, win_ref, gsum_ref, gqk_ref,
                      x1_ref, q_ref, k_ref, v_ref, u_ref):
    x = x_ref[...]
    x1 = x + 0.5 * _swiglu(_rms(x, g1_ref[...]).astype(BF16), wgu_ref, wd_ref)
    x1_ref[...] = x1
    z = _dot(_rms(x1, gmix_ref[...]).astype(BF16), win_ref[...])
    qk = z[:, :QK_WIDTH]
    sq = (qk * qk).astype(BF16)
    ss = jnp.concatenate(
        [_dot(sq[:, lo:lo + NORM_GROUP], gsum_ref[...]) for lo in range(0, ATT_WIDTH, NORM_GROUP)]
        + [_dot(sq[:, ATT_WIDTH:], gsum_ref[:KV_WIDTH, :KV_WIDTH])], axis=1)
    qkn = qk * lax.rsqrt(ss * (1.0 / HEAD_DIM) + EPS) * gqk_ref[...]
    q_ref[...] = qkn[:, :ATT_WIDTH].astype(BF16)
    k_ref[...] = qkn[:, ATT_WIDTH:]
    v_ref[...] = z[:, QK_WIDTH:QK_WIDTH + KV_WIDTH]
    a = z[:, QK_WIDTH + KV_WIDTH:QK_WIDTH + KV_WIDTH + CONV_WIDTH]
    g = z[:, QK_WIDTH + KV_WIDTH + CONV_WIDTH:]
    u = a * jax.nn.sigmoid(g)
    for grp in range(CONV_GROUPS):
        u_ref[pl.ds(grp, u.shape[0], stride=CONV_GROUPS), :] = u[:, grp * LANES:(grp + 1) * LANES]


def _out_ffn2_kernel(x1_ref, mix_ref, wout_ref, g2_ref, wgu_ref, wd_ref, gfin_ref, y_ref):
    x2 = x1_ref[...] + _dot(mix_ref[...], wout_ref[...])
    x3 = x2 + 0.5 * _swiglu(_rms(x2, g2_ref[...]).astype(BF16), wgu_ref, wd_ref)
    y_ref[...] = _rms(x3, gfin_ref[...])


def _zero_of(x):
    bits = pltpu.bitcast(x, jnp.uint32)
    half = jnp.uint32(16)
    return lax.shift_right_logical(lax.shift_right_logical(bits, half), half)


def _after(x, zeros):
    packed = 2 * SUBLANES
    top = pltpu.bitcast(x[0:packed, :], jnp.uint32)
    top = top | jnp.concatenate([zeros] * (x.shape[1] // LANES), axis=1)
    return jnp.concatenate([pltpu.bitcast(top, x.dtype), x[packed:, :]], axis=0)


def _mixer_kernel(*refs, chunk, tile):
    stage, attend, conv = _mixer_parts(pl.program_id(1), *refs, chunk=chunk, piece=chunk, tile=tile,
                                       first_tile_has_no_past=False)
    stage()
    scores, values = attend
    for p in range(tile // chunk):
        values(p, scores(p)[0])
    shift, conv_block, n_blocks = conv
    shift()
    for r in range(n_blocks):
        conv_block(r)


def _mixer_ffn2_kernel(q_ref, kc_ref, kp_ref, vc_ref, vp_ref, uc_ref, up_ref, bias_ref, sink_ref,
                       cw_ref, cb_ref, lng_ref, lnb_ref,
                       x1_ref, wout_ref, g2_ref, wgu_ref, wd_ref, gfin_ref,
                       y_ref, mix_ref, x2_ref, xn_ref, act_ref, kw_ref, vw_ref, ue_ref, us_ref, ys_ref,
                       *, chunk, piece, tile, tiles_per_stream):
    s = pl.program_id(0)

    @pl.when(s == 0)
    def _():
        mix_ref[...] = jnp.zeros_like(mix_ref)

    t = jnp.minimum(s, pl.num_programs(0) - 2)
    stage, attend, conv = _mixer_parts(
        lax.rem(t, tiles_per_stream), q_ref, kc_ref, kp_ref, vc_ref, vp_ref, uc_ref, up_ref, bias_ref,
        sink_ref, cw_ref, cb_ref, lng_ref, lnb_ref, mix_ref, kw_ref, vw_ref, ue_ref, us_ref, ys_ref,
        chunk=chunk, piece=piece, tile=tile, first_tile_has_no_past=True)
    cols = lambda n: slice(n * MXU_TILE, (n + 1) * MXU_TILE)

    n_units, n_out = D_FF // MXU_TILE, D_MODEL // MXU_TILE

    def gate_up(n):
        lo = n * MXU_TILE
        gate = _dot(xn_ref[...], wgu_ref[:, lo:lo + MXU_TILE])
        up = _dot(xn_ref[...], wgu_ref[:, D_FF + lo:D_FF + lo + MXU_TILE])
        act_ref[:, cols(n)] = (gate * jax.nn.sigmoid(gate) * up).astype(BF16)

    def down(n):
        x2_ref[:, cols(n)] = x2_ref[:, cols(n)] + 0.5 * _dot(act_ref[...], wd_ref[:, cols(n)])

    for n in range(n_out):
        x2_ref[:, cols(n)] = x1_ref[:, cols(n)] + _dot(mix_ref[...], wout_ref[:, cols(n)])
    stage()
    xn_ref[...] = _rms(x2_ref[...], g2_ref[...]).astype(BF16)
    scores, values = attend
    units = ([functools.partial(gate_up, n) for n in range(n_units)]
             + [functools.partial(down, n) for n in range(n_out)])
    shift, conv_block, n_blocks = conv
    shift()
    hosts = list(range(len(units) - 1)) + list(range(n_units, len(units) - 1))
    n_pieces = tile // piece
    assert n_pieces <= len(hosts)
    starts = [hosts[:n_pieces].count(n) for n in range(len(units))]
    pieces, blocks = iter(range(n_pieces)), iter(range(n_blocks))
    pending = []
    for n, unit in enumerate(units):
        unit()
        for p, prob in pending:
            values(p, prob)
        pending = []
        for p in itertools.islice(pieces, starts[n]):
            prob, zeros = scores(p)
            for r in itertools.islice(blocks, BLOCKS_PER_PIECE):
                zeros = conv_block(r, zeros)
            pending.append((p, _after(prob, zeros)))
    assert not pending
    y_ref[...] = _rms(x2_ref[...], gfin_ref[...])
    for r in blocks:
        conv_block(r)


def _mixer_parts(i, q_ref, kc_ref, kp_ref, vc_ref, vp_ref, uc_ref, up_ref, bias_ref, sink_ref,
                 cw_ref, cb_ref, lng_ref, lnb_ref, mix_ref, kw_ref, vw_ref, ue_ref, us_ref, ys_ref,
                 *, chunk, piece, tile, first_tile_has_no_past):
    kwin_len = WINDOW + chunk
    n_bias = bias_ref.shape[0]
    per_chunk = chunk // piece

    def scores(p):
        c, part = divmod(p, per_chunk)
        low_half = lax.broadcasted_iota(jnp.int32, (piece, LANES), 1) < HEAD_DIM
        sink = sink_ref[...]
        qc = q_ref[p * piece:(p + 1) * piece, :]
        zero = jnp.zeros_like(qc[:, :LANES])
        blocks = []
        for j in range(N_KV_HEADS):
            keep = low_half if j == 0 else jnp.logical_not(low_half)
            for g in range(GROUP):
                blocks.append(jnp.where(keep, qc[:, g * LANES:(g + 1) * LANES], zero))
        q_big = jnp.concatenate(blocks, axis=0)
        kwin = kw_ref[c * chunk:c * chunk + kwin_len, :]
        s = lax.dot_general(kwin, q_big, (((1,), (1,)), ((), ())), preferred_element_type=F32)
        bidx = jnp.minimum(i * (tile // chunk) + c, n_bias - 1)
        sb = s + bias_ref[bidx, part]
        m = jnp.maximum(jnp.max(sb, axis=0, keepdims=True), sink)
        e = jnp.exp(sb - m)
        denom = jnp.sum(e, axis=0, keepdims=True) + jnp.exp(sink - m)
        return (e * (1.0 / denom)).astype(BF16), _zero_of(s[0:SUBLANES, 0:LANES])

    def values(p, prob):
        c = p // per_chunk
        low_half = lax.broadcasted_iota(jnp.int32, (piece, LANES), 1) < HEAD_DIM
        vwin = vw_ref[c * chunk:c * chunk + kwin_len, :]
        o = lax.dot_general(prob, vwin, (((0,), (0,)), ((), ())), preferred_element_type=F32)
        for g in range(GROUP):
            o0 = o[g * piece:(g + 1) * piece, :]
            o1 = o[(GROUP + g) * piece:(GROUP + g + 1) * piece, :]
            mix_ref[p * piece:(p + 1) * piece, g * LANES:(g + 1) * LANES] = (
                jnp.where(low_half, o0, o1).astype(BF16))

    def stage():
        kw_ref[0:WINDOW, :] = kp_ref[...].astype(BF16)
        kw_ref[WINDOW:, :] = kc_ref[...].astype(BF16)
        vw_ref[0:WINDOW, :] = vp_ref[...].astype(BF16)
        vw_ref[WINDOW:, :] = vc_ref[...].astype(BF16)
        u_prev = up_ref[...]
        if first_tile_has_no_past:
            u_prev = jnp.where(i > 0, u_prev, 0.0)
        past_rows = CONV_GROUPS * CONV_PAD
        ue_ref[0:past_rows, :] = u_prev
        ue_ref[past_rows:, :] = uc_ref[...]

    def shift():
        us_ref[...] = ue_ref[CONV_GROUPS:CONV_GROUPS + us_ref.shape[0], :]

    rows = min(CONV_ROWS, tile)
    block_rows = CONV_GROUPS * rows
    first_tap = CONV_PAD - (CONV_K - 1)

    def conv_block(r, start=None):
        if start is None:
            acc = jnp.zeros((block_rows, LANES), F32)
        else:
            acc = pltpu.bitcast(jnp.concatenate([start] * (block_rows // SUBLANES), axis=0), F32)
        for j in range(CONV_K):
            lo = r * block_rows + CONV_GROUPS * (first_tap + j)
            if lo % SUBLANES == 0:
                taps = ue_ref[lo:lo + block_rows, :]
            else:
                taps = us_ref[lo - CONV_GROUPS:lo - CONV_GROUPS + block_rows, :]
            w = cw_ref[j * SUBLANES:(j + 1) * SUBLANES, :]
            acc = acc + taps * jnp.concatenate([w] * (block_rows // SUBLANES), axis=0)
        ys_ref[...] = acc + jnp.concatenate([cb_ref[...]] * (block_rows // SUBLANES), axis=0)
        y = jnp.concatenate([ys_ref[pl.ds(grp, rows, stride=CONV_GROUPS), :]
                             for grp in range(CONV_GROUPS)], axis=1)
        mu = jnp.mean(y, axis=-1, keepdims=True)
        yc = y - mu
        var = jnp.mean(yc * yc, axis=-1, keepdims=True)
        yn = yc * lax.rsqrt(var + EPS) * lng_ref[...] + lnb_ref[...]
        out = (yn * jax.nn.sigmoid(yn)).astype(BF16)
        mix_ref[r * rows:(r + 1) * rows, ATT_WIDTH:] = out
        packed = 2 * SUBLANES
        zeros = _zero_of(out[0:packed, 0:LANES])
        for lo in range(packed, rows, packed):
            zeros = zeros | _zero_of(out[lo:lo + packed, 0:LANES])
        return zeros

    return stage, (scores, values), (shift, conv_block, tile // rows)


def _const_spec(shape):
    return pl.BlockSpec(shape, lambda *_: (0,) * len(shape), pipeline_mode=pl.Buffered(1))


def _params(n_axes):
    return pltpu.CompilerParams(dimension_semantics=("arbitrary",) * n_axes,
                                vmem_limit_bytes=VMEM_LIMIT)


def _ffn1_proj(x, g1, wgu, wd, gmix, win, gsum, gqk):
    n = x.shape[0]
    tm = min(TOKEN_TILE, n)
    row = lambda w: pl.BlockSpec((tm, w), lambda i: (i, 0))
    return pl.pallas_call(
        _ffn1_proj_kernel,
        grid=(n // tm,),
        in_specs=[row(D_MODEL), _const_spec(g1.shape), _const_spec(wgu.shape), _const_spec(wd.shape),
                  _const_spec(gmix.shape), _const_spec(win.shape), _const_spec(gsum.shape),
                  _const_spec(gqk.shape)],
        out_specs=[row(D_MODEL), row(ATT_WIDTH), row(KV_WIDTH), row(KV_WIDTH),
                   pl.BlockSpec((CONV_GROUPS * tm, LANES), lambda i: (i, 0))],
        out_shape=[jax.ShapeDtypeStruct((n, D_MODEL), F32), jax.ShapeDtypeStruct((n, ATT_WIDTH), BF16),
                   jax.ShapeDtypeStruct((n, KV_WIDTH), F32), jax.ShapeDtypeStruct((n, KV_WIDTH), F32),
                   jax.ShapeDtypeStruct((CONV_GROUPS * n, LANES), F32)],
        compiler_params=_params(1),
        name="ffn1_proj",
    )(x, g1, wgu, wd, gmix, win, gsum, gqk)


def _out_ffn2(x1, mix, wout, g2, wgu, wd, gfin):
    n = x1.shape[0]
    tm = min(TOKEN_TILE, n)
    row = lambda w: pl.BlockSpec((tm, w), lambda i: (i, 0))
    return pl.pallas_call(
        _out_ffn2_kernel,
        grid=(n // tm,),
        in_specs=[row(D_MODEL), row(D_MODEL), _const_spec(wout.shape), _const_spec(g2.shape),
                  _const_spec(wgu.shape), _const_spec(wd.shape), _const_spec(gfin.shape)],
        out_specs=row(D_MODEL),
        out_shape=jax.ShapeDtypeStruct((n, D_MODEL), F32),
        compiler_params=_params(1),
        name="out_ffn2",
    )(x1, mix, wout, g2, wgu, wd, gfin)


def _mixer_scratch(tile):
    return [pltpu.VMEM((WINDOW + tile, KV_WIDTH), BF16),
            pltpu.VMEM((WINDOW + tile, KV_WIDTH), BF16),
            pltpu.VMEM((CONV_GROUPS * (CONV_PAD + tile), LANES), F32),
            pltpu.VMEM((CONV_GROUPS * (CONV_PAD + tile) - SUBLANES, LANES), F32),
            pltpu.VMEM((CONV_GROUPS * min(CONV_ROWS, tile), LANES), F32)]


def _mixer(q, k, v, u, k_past, v_past, u_past, bias, sink_row, cw, cb, lng, lnb,
           *, n_streams, seq, chunk, tile):
    nt = seq // tile
    cur = lambda w: pl.BlockSpec((tile, w), lambda b, i: (b * nt + i, 0))
    kv_past = pl.BlockSpec((WINDOW, KV_WIDTH), lambda b, i: (b, 0))
    u_past_spec = pl.BlockSpec((CONV_GROUPS * CONV_PAD, LANES), lambda b, i: (b, 0))
    body = functools.partial(_mixer_kernel, chunk=chunk, tile=tile)
    return pl.pallas_call(
        body,
        grid=(n_streams, nt),
        in_specs=[cur(ATT_WIDTH), cur(KV_WIDTH), kv_past, cur(KV_WIDTH), kv_past,
                  pl.BlockSpec((CONV_GROUPS * tile, LANES), lambda b, i: (b * nt + i, 0)),
                  u_past_spec, _const_spec(bias.shape), _const_spec(sink_row.shape),
                  _const_spec(cw.shape), _const_spec(cb.shape), _const_spec(lng.shape),
                  _const_spec(lnb.shape)],
        out_specs=cur(D_MODEL),
        out_shape=jax.ShapeDtypeStruct((n_streams * seq, D_MODEL), BF16),
        scratch_shapes=_mixer_scratch(tile),
        compiler_params=_params(2),
        name="mixer",
    )(q, k, k_past, v, v_past, u, u_past, bias, sink_row, cw, cb, lng, lnb)


def _mixer_ffn2(q, k, v, u, bias, sink_row, cw, cb, lng, lnb, x1, wout, g2, wgu, wd, gfin,
                *, seq, chunk, tile):
    n_tiles = q.shape[0] // tile
    mix_tile = lambda s: jnp.minimum(s, n_tiles - 1)
    ffn_tile = lambda s: jnp.maximum(s - 1, 0)
    cur = lambda w: pl.BlockSpec((tile, w), lambda s: (mix_tile(s), 0))
    kv_past = pl.BlockSpec(
        (WINDOW, KV_WIDTH), lambda s: (jnp.maximum(mix_tile(s) * (tile // WINDOW) - 1, 0), 0))
    u_past = pl.BlockSpec(
        (CONV_GROUPS * CONV_PAD, LANES),
        lambda s: (jnp.maximum(mix_tile(s) * (tile // CONV_PAD) - 1, 0), 0))
    ffn_row = pl.BlockSpec((tile, D_MODEL), lambda s: (ffn_tile(s), 0))
    body = functools.partial(_mixer_ffn2_kernel, chunk=chunk, piece=ATT_PIECE, tile=tile,
                             tiles_per_stream=seq // tile)
    return pl.pallas_call(
        body,
        grid=(n_tiles + 1,),
        in_specs=[cur(ATT_WIDTH), cur(KV_WIDTH), kv_past, cur(KV_WIDTH), kv_past,
                  pl.BlockSpec((CONV_GROUPS * tile, LANES), lambda s: (mix_tile(s), 0)),
                  u_past, _const_spec(bias.shape), _const_spec(sink_row.shape),
                  _const_spec(cw.shape), _const_spec(cb.shape), _const_spec(lng.shape),
                  _const_spec(lnb.shape),
                  ffn_row, _const_spec(wout.shape), _const_spec(g2.shape), _const_spec(wgu.shape),
                  _const_spec(wd.shape), _const_spec(gfin.shape)],
        out_specs=ffn_row,
        out_shape=jax.ShapeDtypeStruct((q.shape[0], D_MODEL), F32),
        scratch_shapes=[pltpu.VMEM((tile, D_MODEL), BF16), pltpu.VMEM((tile, D_MODEL), F32),
                        pltpu.VMEM((tile, D_MODEL), BF16), pltpu.VMEM((tile, D_FF), BF16)]
        + _mixer_scratch(tile),
        compiler_params=_params(1),
        name="mixer_ffn2",
    )(q, k, k, v, v, u, u, bias, sink_row, cw, cb, lng, lnb, x1, wout, g2, wgu, wd, gfin)


def _rel_bucket(rel):
    nb = NUM_BUCKETS // 2
    max_exact = nb // 2
    ret = np.where(rel > 0, nb, 0)
    n = np.abs(rel)
    nf = np.maximum(n, 1).astype(np.float32)
    large = max_exact + (np.log(nf / np.float32(max_exact)) / np.float32(math.log(MAX_DISTANCE / max_exact))
                         * np.float32(nb - max_exact)).astype(np.int32)
    large = np.minimum(large, nb - 1)
    return (ret + np.where(n < max_exact, n, large)).astype(np.int32)


def _bias_cols(table, qpos, kpos, piece):
    bucket = _rel_bucket(kpos[:, None] - qpos[None, :])
    onehot = jnp.asarray(bucket[:, :, None] == np.arange(NUM_BUCKETS), dtype=F32)
    b = jnp.einsum("kqb,bh->khq", onehot, table.astype(F32), precision=lax.Precision.HIGHEST)
    b = b.reshape(kpos.shape[0], N_HEADS, qpos.shape[0] // piece, piece)
    return jnp.transpose(b, (2, 0, 1, 3)).reshape(-1, kpos.shape[0], N_HEADS * piece)


def kernel(x_prompt, x_sample, cache_k, cache_v, state_conv, rel_bias_table, ffn1_norm, ffn1_w_gu,
           ffn1_w_down, mix_norm, w_in, q_norm, k_norm, sinks, conv_w, conv_b, conv_ln_g, conv_ln_b,
           w_out, ffn2_norm, ffn2_w_gu, ffn2_w_down, final_norm):
    B, S, _ = x_prompt.shape
    DB, DS, _ = x_sample.shape
    R = cache_k.shape[2]
    past_len = R
    assert ffn1_norm.shape[0] == 1 and R == WINDOW and S % MIX_TILE == 0 and DS % 8 == 0
    l = 0
    row = lambda a: a.reshape(1, -1).astype(F32)

    pair = lambda a, axis: jnp.swapaxes(
        a.reshape(a.shape[:axis] + (N_KV_HEADS, GROUP, HEAD_DIM) + a.shape[axis + 1:]),
        axis, axis + 1).reshape(a.shape)
    win = jnp.concatenate([pair(w_in[l][:, :ATT_WIDTH], 1), w_in[l][:, ATT_WIDTH:]], axis=1).astype(BF16)
    wout = jnp.concatenate([pair(w_out[l][:ATT_WIDTH], 0), w_out[l][ATT_WIDTH:]], axis=0).astype(BF16)
    gqk = jnp.concatenate([jnp.tile(q_norm[l], N_HEADS) * (HEAD_DIM ** -0.5),
                           jnp.tile(k_norm[l], N_KV_HEADS)]).reshape(1, QK_WIDTH).astype(F32)
    head_of_col = np.arange(NORM_GROUP) // HEAD_DIM
    gsum = jnp.asarray(head_of_col[:, None] == head_of_col[None, :], dtype=BF16)
    wgu1, wd1 = ffn1_w_gu[l].astype(BF16), ffn1_w_down[l].astype(BF16)
    wgu2, wd2 = ffn2_w_gu[l].astype(BF16), ffn2_w_down[l].astype(BF16)
    by_group = lambda a: jnp.tile(a.astype(F32).reshape(-1, CONV_GROUPS, LANES),
                                  (1, SUBLANES // CONV_GROUPS, 1)).reshape(-1, LANES)
    cw, cb = by_group(conv_w[l]), by_group(conv_b[l])
    lng, lnb = row(conv_ln_g[l]), row(conv_ln_b[l])

    band = WINDOW + CHUNK
    bias_p = _bias_cols(rel_bias_table, np.arange(CHUNK), np.arange(band) - WINDOW, ATT_PIECE)
    key = np.arange(band)[None, :, None]
    neg = lambda first_valid: jnp.where(key >= first_valid, bias_p, -jnp.inf)
    bias_prompt = jnp.stack([neg(WINDOW), neg(WINDOW - CHUNK), bias_p])
    qpos_s = past_len + np.arange(DS)
    kpos_s = np.concatenate([past_len - R + np.arange(R), qpos_s])
    bias_sample = _bias_cols(rel_bias_table, qpos_s, kpos_s, DS)[None]
    sink_row = lambda n: jnp.broadcast_to(sinks[l].astype(F32)[:, None], (N_HEADS, n)).reshape(1, N_HEADS * n)

    def layer(x, n_streams, seq, chunk, tile, past):
        x1, q, k, v, u = _ffn1_proj(x, row(ffn1_norm[l]), wgu1, wd1, row(mix_norm[l]), win, gsum, gqk)
        if past is None:
            y = _mixer_ffn2(q, k, v, u, bias_prompt, sink_row(ATT_PIECE), cw, cb, lng, lnb, x1, wout,
                            row(ffn2_norm[l]), wgu2, wd2, row(final_norm[l]),
                            seq=seq, chunk=chunk, tile=tile)
        else:
            mix = _mixer(q, k, v, u, *past, bias_sample, sink_row(chunk), cw, cb, lng, lnb,
                         n_streams=n_streams, seq=seq, chunk=chunk, tile=tile)
            y = _out_ffn2(x1, mix, wout, row(ffn2_norm[l]), wgu2, wd2, row(final_norm[l]))
        return y, k, v, u

    yp, kp, vp, up = layer(x_prompt.reshape(B * S, D_MODEL), B, S, CHUNK, MIX_TILE, None)
    ck = cache_k[l].reshape(DB * R, KV_WIDTH).astype(F32)
    cv = cache_v[l].reshape(DB * R, KV_WIDTH).astype(F32)
    sc = jnp.pad(state_conv[l].astype(F32), ((0, 0), (CONV_PAD - (CONV_K - 1), 0), (0, 0)))
    ys, ks, vs, us = layer(x_sample.reshape(DB * DS, D_MODEL), DB, DS, DS, DS,
                           (ck, cv, sc.reshape(DB * CONV_PAD * CONV_GROUPS, LANES)))

    RP = min(WINDOW, S)
    heads = lambda a, n, t: a.reshape(n, t, N_KV_HEADS, HEAD_DIM)
    tail = lambda a, n: a.reshape(B, -1, a.shape[-1])[:, -n * (a.shape[0] // (B * S)):]
    new_k_prompt = heads(tail(kp, RP), B, RP)[None]
    new_v_prompt = heads(tail(vp, RP), B, RP)[None]
    new_conv_prompt = tail(up, CONV_K - 1).reshape(B, CONV_K - 1, CONV_WIDTH)[None]
    new_k_sample = jnp.concatenate([cache_k[l].astype(F32), heads(ks, DB, DS)], axis=1)[:, DS:][None]
    new_v_sample = jnp.concatenate([cache_v[l].astype(F32), heads(vs, DB, DS)], axis=1)[:, DS:][None]
    new_conv_sample = jnp.concatenate([state_conv[l].astype(F32), us.reshape(DB, DS, CONV_WIDTH)],
                                      axis=1)[:, DS:][None]
    return (yp.reshape(B, S, D_MODEL), ys.reshape(DB, DS, D_MODEL), new_k_prompt, new_v_prompt,
            new_conv_prompt, new_k_sample, new_v_sample, new_conv_sample)
```

```python
import functools
import itertools
import math

import numpy as np
import jax
import jax.numpy as jnp
from jax import lax
from jax.experimental import pallas as pl
from jax.experimental.pallas import tpu as pltpu

D_MODEL = 1024
CHUNK = 64
HEAD_DIM = 64
N_HEADS = 8
N_KV_HEADS = 2
GROUP = N_HEADS // N_KV_HEADS
ATT_WIDTH = N_HEADS * HEAD_DIM
KV_WIDTH = N_KV_HEADS * HEAD_DIM
CONV_WIDTH = D_MODEL - ATT_WIDTH
CONV_K = 31
WINDOW = 128
IN_WIDTH = ATT_WIDTH + 2 * KV_WIDTH + 2 * CONV_WIDTH
QK_WIDTH = ATT_WIDTH + KV_WIDTH
D_FF = 2816
NUM_BUCKETS = 32
MAX_DISTANCE = 128
EPS = 1e-6

LANES = 128
CONV_PAD = 32
FF_SPLITS = (0, 1536, D_FF)
SUBLANES = 8
MXU_TILE = 256
NORM_GROUP = MXU_TILE
CONV_ROWS = 16
CONV_GROUPS = CONV_WIDTH // LANES
TOKEN_TILE = 1024
MIX_TILE = 512
ATT_PIECE = 64
BLOCKS_PER_PIECE = 1
VMEM_LIMIT = 56 * 1024 * 1024

F32 = jnp.float32
BF16 = jnp.bfloat16


def _dot(a, b):
    return jnp.dot(a, b, preferred_element_type=F32)


def _rms(x, g):
    ms = jnp.mean(x * x, axis=-1, keepdims=True)
    return x * lax.rsqrt(ms + EPS) * g


def _inv_rms(x):
    return lax.rsqrt(jnp.mean(x * x, axis=-1, keepdims=True) + EPS)


def _swiglu(xn, wgu_ref, wd_ref, row_scale=None):
    acc = None
    for lo, hi in zip(FF_SPLITS[:-1], FF_SPLITS[1:]):
        gate = _dot(xn, wgu_ref[:, lo:hi])
        up = _dot(xn, wgu_ref[:, D_FF + lo:D_FF + hi])
        if row_scale is not None:
            gate, up = gate * row_scale, up * row_scale
        act = (gate * jax.nn.sigmoid(gate) * up).astype(BF16)
        part = _dot(act, wd_ref[lo:hi, :])
        acc = part if acc is None else acc + part
    return acc


def _ffn1_proj_kernel(x_ref, g1_ref, wgu_ref, wd_ref, gmix_ref, win_ref, gsum_ref, gqk_ref,
                      x1_ref, q_ref, k_ref, v_ref, u_ref):
    x = x_ref[...]
    x1 = x + 0.5 * _swiglu((x * g1_ref[...]).astype(BF16), wgu_ref, wd_ref, _inv_rms(x))
    x1_ref[...] = x1
    z = _dot((x1 * gmix_ref[...]).astype(BF16), win_ref[...]) * _inv_rms(x1)
    qk = z[:, :QK_WIDTH]
    sq = (qk * qk).astype(BF16)
    ss = jnp.concatenate(
        [_dot(sq[:, lo:lo + NORM_GROUP], gsum_ref[...]) for lo in range(0, ATT_WIDTH, NORM_GROUP)]
        + [_dot(sq[:, ATT_WIDTH:], gsum_ref[:KV_WIDTH, :KV_WIDTH])], axis=1)
    qkn = qk * lax.rsqrt(ss * (1.0 / HEAD_DIM) + EPS) * gqk_ref[...]
    q_ref[...] = qkn[:, :ATT_WIDTH].astype(BF16)
    k_ref[...] = qkn[:, ATT_WIDTH:]
    v_ref[...] = z[:, QK_WIDTH:QK_WIDTH + KV_WIDTH]
    a = z[:, QK_WIDTH + KV_WIDTH:QK_WIDTH + KV_WIDTH + CONV_WIDTH]
    g = z[:, QK_WIDTH + KV_WIDTH + CONV_WIDTH:]
    u = a * jax.nn.sigmoid(g)
    for grp in range(CONV_GROUPS):
        u_ref[pl.ds(grp, u.shape[0], stride=CONV_GROUPS), :] = u[:, grp * LANES:(grp + 1) * LANES]


def _out_ffn2_kernel(x1_ref, mix_ref, wout_ref, g2_ref, wgu_ref, wd_ref, gfin_ref, y_ref):
    x2 = x1_ref[...] + _dot(mix_ref[...], wout_ref[...])
    x3 = x2 + 0.5 * _swiglu((x2 * g2_ref[...]).astype(BF16), wgu_ref, wd_ref, _inv_rms(x2))
    y_ref[...] = _rms(x3, gfin_ref[...])


def _zero_of(x):
    bits = pltpu.bitcast(x, jnp.uint32)
    half = jnp.uint32(16)
    return lax.shift_right_logical(lax.shift_right_logical(bits, half), half)


def _after(x, zeros):
    packed = 2 * SUBLANES
    top = pltpu.bitcast(x[0:packed, :], jnp.uint32)
    top = top | jnp.concatenate([zeros] * (x.shape[1] // LANES), axis=1)
    return jnp.concatenate([pltpu.bitcast(top, x.dtype), x[packed:, :]], axis=0)


def _mixer_kernel(*refs, chunk, tile):
    stage, attend, conv = _mixer_parts(pl.program_id(1), *refs, chunk=chunk, piece=chunk, tile=tile,
                                       first_tile_has_no_past=False)
    stage()
    scores, values = attend
    for p in range(tile // chunk):
        values(p, scores(p)[0])
    shift, conv_block, n_blocks = conv
    shift()
    for r in range(n_blocks):
        conv_block(r)


def _mixer_ffn2_kernel(q_ref, kc_ref, kp_ref, vc_ref, vp_ref, uc_ref, up_ref, bias_ref, sink_ref,
                       cw_ref, cb_ref, lng_ref, lnb_ref,
                       x1_ref, wout_ref, g2_ref, wgu_ref, wd_ref, gfin_ref,
                       y_ref, mix_ref, x2_ref, xn_ref, act_ref, kw_ref, vw_ref, ue_ref, us_ref, ys_ref,
                       *, chunk, piece, tile, tiles_per_stream):
    s = pl.program_id(0)

    @pl.when(s == 0)
    def _():
        mix_ref[...] = jnp.zeros_like(mix_ref)

    t = jnp.minimum(s, pl.num_programs(0) - 2)
    stage, attend, conv = _mixer_parts(
        lax.rem(t, tiles_per_stream), q_ref, kc_ref, kp_ref, vc_ref, vp_ref, uc_ref, up_ref, bias_ref,
        sink_ref, cw_ref, cb_ref, lng_ref, lnb_ref, mix_ref, kw_ref, vw_ref, ue_ref, us_ref, ys_ref,
        chunk=chunk, piece=piece, tile=tile, first_tile_has_no_past=True)
    cols = lambda n: slice(n * MXU_TILE, (n + 1) * MXU_TILE)

    n_units, n_out = D_FF // MXU_TILE, D_MODEL // MXU_TILE

    def gate_up(n):
        lo = n * MXU_TILE
        gate = _dot(xn_ref[...], wgu_ref[:, lo:lo + MXU_TILE])
        up = _dot(xn_ref[...], wgu_ref[:, D_FF + lo:D_FF + lo + MXU_TILE])
        act_ref[:, cols(n)] = (gate * jax.nn.sigmoid(gate) * up).astype(BF16)

    def down(n):
        x2_ref[:, cols(n)] = x2_ref[:, cols(n)] + 0.5 * _dot(act_ref[...], wd_ref[:, cols(n)])

    for n in range(n_out):
        x2_ref[:, cols(n)] = x1_ref[:, cols(n)] + _dot(mix_ref[...], wout_ref[:, cols(n)])
    stage()
    xn_ref[...] = _rms(x2_ref[...], g2_ref[...]).astype(BF16)
    scores, values = attend
    units = ([functools.partial(gate_up, n) for n in range(n_units)]
             + [functools.partial(down, n) for n in range(n_out)])
    shift, conv_block, n_blocks = conv
    shift()
    hosts = list(range(len(units) - 1)) + list(range(n_units, len(units) - 1))
    n_pieces = tile // piece
    assert n_pieces <= len(hosts)
    starts = [hosts[:n_pieces].count(n) for n in range(len(units))]
    pieces, blocks = iter(range(n_pieces)), iter(range(n_blocks))
    pending = []
    for n, unit in enumerate(units):
        unit()
        for p, prob in pending:
            values(p, prob)
        pending = []
        for p in itertools.islice(pieces, starts[n]):
            prob, zeros = scores(p)
            for r in itertools.islice(blocks, BLOCKS_PER_PIECE):
                zeros = conv_block(r, zeros)
            pending.append((p, _after(prob, zeros)))
    assert not pending
    y_ref[...] = _rms(x2_ref[...], gfin_ref[...])
    for r in blocks:
        conv_block(r)


def _mixer_parts(i, q_ref, kc_ref, kp_ref, vc_ref, vp_ref, uc_ref, up_ref, bias_ref, sink_ref,
                 cw_ref, cb_ref, lng_ref, lnb_ref, mix_ref, kw_ref, vw_ref, ue_ref, us_ref, ys_ref,
                 *, chunk, piece, tile, first_tile_has_no_past):
    kwin_len = WINDOW + chunk
    n_bias = bias_ref.shape[0]
    per_chunk = chunk // piece

    def scores(p):
        c, part = divmod(p, per_chunk)
        low_half = lax.broadcasted_iota(jnp.int32, (piece, LANES), 1) < HEAD_DIM
        sink = sink_ref[...]
        qc = q_ref[p * piece:(p + 1) * piece, :]
        zero = jnp.zeros_like(qc[:, :LANES])
        blocks = []
        for j in range(N_KV_HEADS):
            keep = low_half if j == 0 else jnp.logical_not(low_half)
            for g in range(GROUP):
                blocks.append(jnp.where(keep, qc[:, g * LANES:(g + 1) * LANES], zero))
        q_big = jnp.concatenate(blocks, axis=0)
        kwin = kw_ref[c * chunk:c * chunk + kwin_len, :]
        s = lax.dot_general(kwin, q_big, (((1,), (1,)), ((), ())), preferred_element_type=F32)
        bidx = jnp.minimum(i * (tile // chunk) + c, n_bias - 1)
        sb = s + bias_ref[bidx, part]
        m = jnp.maximum(jnp.max(sb, axis=0, keepdims=True), sink)
        e = jnp.exp(sb - m)
        denom = jnp.sum(e, axis=0, keepdims=True) + jnp.exp(sink - m)
        return (e * (1.0 / denom)).astype(BF16), _zero_of(s[0:SUBLANES, 0:LANES])

    def values(p, prob):
        c = p // per_chunk
        low_half = lax.broadcasted_iota(jnp.int32, (piece, LANES), 1) < HEAD_DIM
        vwin = vw_ref[c * chunk:c * chunk + kwin_len, :]
        o = lax.dot_general(prob, vwin, (((0,), (0,)), ((), ())), preferred_element_type=F32)
        for g in range(GROUP):
            o0 = o[g * piece:(g + 1) * piece, :]
            o1 = o[(GROUP + g) * piece:(GROUP + g + 1) * piece, :]
            mix_ref[p * piece:(p + 1) * piece, g * LANES:(g + 1) * LANES] = (
                jnp.where(low_half, o0, o1).astype(BF16))

    def stage():
        kw_ref[0:WINDOW, :] = kp_ref[...].astype(BF16)
        kw_ref[WINDOW:, :] = kc_ref[...].astype(BF16)
        vw_ref[0:WINDOW, :] = vp_ref[...].astype(BF16)
        vw_ref[WINDOW:, :] = vc_ref[...].astype(BF16)
        u_prev = up_ref[...]
        if first_tile_has_no_past:
            u_prev = jnp.where(i > 0, u_prev, 0.0)
        past_rows = CONV_GROUPS * CONV_PAD
        ue_ref[0:past_rows, :] = u_prev
        ue_ref[past_rows:, :] = uc_ref[...]

    def shift():
        us_ref[...] = ue_ref[CONV_GROUPS:CONV_GROUPS + us_ref.shape[0], :]

    rows = min(CONV_ROWS, tile)
    block_rows = CONV_GROUPS * rows
    first_tap = CONV_PAD - (CONV_K - 1)

    def conv_block(r, start=None):
        if start is None:
            acc = jnp.zeros((block_rows, LANES), F32)
        else:
            acc = pltpu.bitcast(jnp.concatenate([start] * (block_rows // SUBLANES), axis=0), F32)
        for j in range(CONV_K):
            lo = r * block_rows + CONV_GROUPS * (first_tap + j)
            if lo % SUBLANES == 0:
                taps = ue_ref[lo:lo + block_rows, :]
            else:
                taps = us_ref[lo - CONV_GROUPS:lo - CONV_GROUPS + block_rows, :]
            w = cw_ref[j * SUBLANES:(j + 1) * SUBLANES, :]
            acc = acc + taps * jnp.concatenate([w] * (block_rows // SUBLANES), axis=0)
        ys_ref[...] = acc + jnp.concatenate([cb_ref[...]] * (block_rows // SUBLANES), axis=0)
        y = jnp.concatenate([ys_ref[pl.ds(grp, rows, stride=CONV_GROUPS), :]
                             for grp in range(CONV_GROUPS)], axis=1)
        mu = jnp.mean(y, axis=-1, keepdims=True)
        yc = y - mu
        var = jnp.mean(yc * yc, axis=-1, keepdims=True)
        yn = yc * lax.rsqrt(var + EPS) * lng_ref[...] + lnb_ref[...]
        out = (yn * jax.nn.sigmoid(yn)).astype(BF16)
        mix_ref[r * rows:(r + 1) * rows, ATT_WIDTH:] = out
        packed = 2 * SUBLANES
        zeros = _zero_of(out[0:packed, 0:LANES])
        for lo in range(packed, rows, packed):
            zeros = zeros | _zero_of(out[lo:lo + packed, 0:LANES])
        return zeros

    return stage, (scores, values), (shift, conv_block, tile // rows)


def _const_spec(shape):
    return pl.BlockSpec(shape, lambda *_: (0,) * len(shape), pipeline_mode=pl.Buffered(1))


def _params(n_axes):
    return pltpu.CompilerParams(dimension_semantics=("arbitrary",) * n_axes,
                                vmem_limit_bytes=VMEM_LIMIT)


def _ffn1_proj(x, g1, wgu, wd, gmix, win, gsum, gqk):
    n = x.shape[0]
    tm = min(TOKEN_TILE, n)
    row = lambda w: pl.BlockSpec((tm, w), lambda i: (i, 0))
    return pl.pallas_call(
        _ffn1_proj_kernel,
        grid=(n // tm,),
        in_specs=[row(D_MODEL), _const_spec(g1.shape), _const_spec(wgu.shape), _const_spec(wd.shape),
                  _const_spec(gmix.shape), _const_spec(win.shape), _const_spec(gsum.shape),
                  _const_spec(gqk.shape)],
        out_specs=[row(D_MODEL), row(ATT_WIDTH), row(KV_WIDTH), row(KV_WIDTH),
                   pl.BlockSpec((CONV_GROUPS * tm, LANES), lambda i: (i, 0))],
        out_shape=[jax.ShapeDtypeStruct((n, D_MODEL), F32), jax.ShapeDtypeStruct((n, ATT_WIDTH), BF16),
                   jax.ShapeDtypeStruct((n, KV_WIDTH), F32), jax.ShapeDtypeStruct((n, KV_WIDTH), F32),
                   jax.ShapeDtypeStruct((CONV_GROUPS * n, LANES), F32)],
        compiler_params=_params(1),
        name="ffn1_proj",
    )(x, g1, wgu, wd, gmix, win, gsum, gqk)


def _out_ffn2(x1, mix, wout, g2, wgu, wd, gfin):
    n = x1.shape[0]
    tm = min(TOKEN_TILE, n)
    row = lambda w: pl.BlockSpec((tm, w), lambda i: (i, 0))
    return pl.pallas_call(
        _out_ffn2_kernel,
        grid=(n // tm,),
        in_specs=[row(D_MODEL), row(D_MODEL), _const_spec(wout.shape), _const_spec(g2.shape),
                  _const_spec(wgu.shape), _const_spec(wd.shape), _const_spec(gfin.shape)],
        out_specs=row(D_MODEL),
        out_shape=jax.ShapeDtypeStruct((n, D_MODEL), F32),
        compiler_params=_params(1),
        name="out_ffn2",
    )(x1, mix, wout, g2, wgu, wd, gfin)


def _mixer_scratch(tile):
    return [pltpu.VMEM((WINDOW + tile, KV_WIDTH), BF16),
            pltpu.VMEM((WINDOW + tile, KV_WIDTH), BF16),
            pltpu.VMEM((CONV_GROUPS * (CONV_PAD + tile), LANES), F32),
            pltpu.VMEM((CONV_GROUPS * (CONV_PAD + tile) - SUBLANES, LANES), F32),
            pltpu.VMEM((CONV_GROUPS * min(CONV_ROWS, tile), LANES), F32)]


def _mixer(q, k, v, u, k_past, v_past, u_past, bias, sink_row, cw, cb, lng, lnb,
           *, n_streams, seq, chunk, tile):
    nt = seq // tile
    cur = lambda w: pl.BlockSpec((tile, w), lambda b, i: (b * nt + i, 0))
    kv_past = pl.BlockSpec((WINDOW, KV_WIDTH), lambda b, i: (b, 0))
    u_past_spec = pl.BlockSpec((CONV_GROUPS * CONV_PAD, LANES), lambda b, i: (b, 0))
    body = functools.partial(_mixer_kernel, chunk=chunk, tile=tile)
    return pl.pallas_call(
        body,
        grid=(n_streams, nt),
        in_specs=[cur(ATT_WIDTH), cur(KV_WIDTH), kv_past, cur(KV_WIDTH), kv_past,
                  pl.BlockSpec((CONV_GROUPS * tile, LANES), lambda b, i: (b * nt + i, 0)),
                  u_past_spec, _const_spec(bias.shape), _const_spec(sink_row.shape),
                  _const_spec(cw.shape), _const_spec(cb.shape), _const_spec(lng.shape),
                  _const_spec(lnb.shape)],
        out_specs=cur(D_MODEL),
        out_shape=jax.ShapeDtypeStruct((n_streams * seq, D_MODEL), BF16),
        scratch_shapes=_mixer_scratch(tile),
        compiler_params=_params(2),
        name="mixer",
    )(q, k, k_past, v, v_past, u, u_past, bias, sink_row, cw, cb, lng, lnb)


def _mixer_ffn2(q, k, v, u, bias, sink_row, cw, cb, lng, lnb, x1, wout, g2, wgu, wd, gfin,
                *, seq, chunk, tile):
    n_tiles = q.shape[0] // tile
    mix_tile = lambda s: jnp.minimum(s, n_tiles - 1)
    ffn_tile = lambda s: jnp.maximum(s - 1, 0)
    cur = lambda w: pl.BlockSpec((tile, w), lambda s: (mix_tile(s), 0))
    kv_past = pl.BlockSpec(
        (WINDOW, KV_WIDTH), lambda s: (jnp.maximum(mix_tile(s) * (tile // WINDOW) - 1, 0), 0))
    u_past = pl.BlockSpec(
        (CONV_GROUPS * CONV_PAD, LANES),
        lambda s: (jnp.maximum(mix_tile(s) * (tile // CONV_PAD) - 1, 0), 0))
    ffn_row = pl.BlockSpec((tile, D_MODEL), lambda s: (ffn_tile(s), 0))
    body = functools.partial(_mixer_ffn2_kernel, chunk=chunk, piece=ATT_PIECE, tile=tile,
                             tiles_per_stream=seq // tile)
    return pl.pallas_call(
        body,
        grid=(n_tiles + 1,),
        in_specs=[cur(ATT_WIDTH), cur(KV_WIDTH), kv_past, cur(KV_WIDTH), kv_past,
                  pl.BlockSpec((CONV_GROUPS * tile, LANES), lambda s: (mix_tile(s), 0)),
                  u_past, _const_spec(bias.shape), _const_spec(sink_row.shape),
                  _const_spec(cw.shape), _const_spec(cb.shape), _const_spec(lng.shape),
                  _const_spec(lnb.shape),
                  ffn_row, _const_spec(wout.shape), _const_spec(g2.shape), _const_spec(wgu.shape),
                  _const_spec(wd.shape), _const_spec(gfin.shape)],
        out_specs=ffn_row,
        out_shape=jax.ShapeDtypeStruct((q.shape[0], D_MODEL), F32),
        scratch_shapes=[pltpu.VMEM((tile, D_MODEL), BF16), pltpu.VMEM((tile, D_MODEL), F32),
                        pltpu.VMEM((tile, D_MODEL), BF16), pltpu.VMEM((tile, D_FF), BF16)]
        + _mixer_scratch(tile),
        compiler_params=_params(1),
        name="mixer_ffn2",
    )(q, k, k, v, v, u, u, bias, sink_row, cw, cb, lng, lnb, x1, wout, g2, wgu, wd, gfin)


def _rel_bucket(rel):
    nb = NUM_BUCKETS // 2
    max_exact = nb // 2
    ret = np.where(rel > 0, nb, 0)
    n = np.abs(rel)
    nf = np.maximum(n, 1).astype(np.float32)
    large = max_exact + (np.log(nf / np.float32(max_exact)) / np.float32(math.log(MAX_DISTANCE / max_exact))
                         * np.float32(nb - max_exact)).astype(np.int32)
    large = np.minimum(large, nb - 1)
    return (ret + np.where(n < max_exact, n, large)).astype(np.int32)


def _bias_cols(table, qpos, kpos, piece):
    bucket = _rel_bucket(kpos[:, None] - qpos[None, :])
    onehot = jnp.asarray(bucket[:, :, None] == np.arange(NUM_BUCKETS), dtype=F32)
    b = jnp.einsum("kqb,bh->khq", onehot, table.astype(F32), precision=lax.Precision.HIGHEST)
    b = b.reshape(kpos.shape[0], N_HEADS, qpos.shape[0] // piece, piece)
    return jnp.transpose(b, (2, 0, 1, 3)).reshape(-1, kpos.shape[0], N_HEADS * piece)


def kernel(x_prompt, x_sample, cache_k, cache_v, state_conv, rel_bias_table, ffn1_norm, ffn1_w_gu,
           ffn1_w_down, mix_norm, w_in, q_norm, k_norm, sinks, conv_w, conv_b, conv_ln_g, conv_ln_b,
           w_out, ffn2_norm, ffn2_w_gu, ffn2_w_down, final_norm):
    B, S, _ = x_prompt.shape
    DB, DS, _ = x_sample.shape
    R = cache_k.shape[2]
    past_len = R
    assert ffn1_norm.shape[0] == 1 and R == WINDOW and S % MIX_TILE == 0 and DS % 8 == 0
    l = 0
    row = lambda a: a.reshape(1, -1).astype(F32)

    pair = lambda a, axis: jnp.swapaxes(
        a.reshape(a.shape[:axis] + (N_KV_HEADS, GROUP, HEAD_DIM) + a.shape[axis + 1:]),
        axis, axis + 1).reshape(a.shape)
    win = jnp.concatenate([pair(w_in[l][:, :ATT_WIDTH], 1), w_in[l][:, ATT_WIDTH:]], axis=1).astype(BF16)
    wout = jnp.concatenate([pair(w_out[l][:ATT_WIDTH], 0), w_out[l][ATT_WIDTH:]], axis=0).astype(BF16)
    gqk = jnp.concatenate([jnp.tile(q_norm[l], N_HEADS) * (HEAD_DIM ** -0.5),
                           jnp.tile(k_norm[l], N_KV_HEADS)]).reshape(1, QK_WIDTH).astype(F32)
    head_of_col = np.arange(NORM_GROUP) // HEAD_DIM
    gsum = jnp.asarray(head_of_col[:, None] == head_of_col[None, :], dtype=BF16)
    wgu1, wd1 = ffn1_w_gu[l].astype(BF16), ffn1_w_down[l].astype(BF16)
    wgu2, wd2 = ffn2_w_gu[l].astype(BF16), ffn2_w_down[l].astype(BF16)
    by_group = lambda a: jnp.tile(a.astype(F32).reshape(-1, CONV_GROUPS, LANES),
                                  (1, SUBLANES // CONV_GROUPS, 1)).reshape(-1, LANES)
    cw, cb = by_group(conv_w[l]), by_group(conv_b[l])
    lng, lnb = row(conv_ln_g[l]), row(conv_ln_b[l])

    band = WINDOW + CHUNK
    bias_p = _bias_cols(rel_bias_table, np.arange(CHUNK), np.arange(band) - WINDOW, ATT_PIECE)
    key = np.arange(band)[None, :, None]
    neg = lambda first_valid: jnp.where(key >= first_valid, bias_p, -jnp.inf)
    bias_prompt = jnp.stack([neg(WINDOW), neg(WINDOW - CHUNK), bias_p])
    qpos_s = past_len + np.arange(DS)
    kpos_s = np.concatenate([past_len - R + np.arange(R), qpos_s])
    bias_sample = _bias_cols(rel_bias_table, qpos_s, kpos_s, DS)[None]
    sink_row = lambda n: jnp.broadcast_to(sinks[l].astype(F32)[:, None], (N_HEADS, n)).reshape(1, N_HEADS * n)

    def layer(x, n_streams, seq, chunk, tile, past):
        x1, q, k, v, u = _ffn1_proj(x, row(ffn1_norm[l]), wgu1, wd1, row(mix_norm[l]), win, gsum, gqk)
        if past is None:
            y = _mixer_ffn2(q, k, v, u, bias_prompt, sink_row(ATT_PIECE), cw, cb, lng, lnb, x1, wout,
                            row(ffn2_norm[l]), wgu2, wd2, row(final_norm[l]),
                            seq=seq, chunk=chunk, tile=tile)
        else:
            mix = _mixer(q, k, v, u, *past, bias_sample, sink_row(chunk), cw, cb, lng, lnb,
                         n_streams=n_streams, seq=seq, chunk=chunk, tile=tile)
            y = _out_ffn2(x1, mix, wout, row(ffn2_norm[l]), wgu2, wd2, row(final_norm[l]))
        return y, k, v, u

    yp, kp, vp, up = layer(x_prompt.reshape(B * S, D_MODEL), B, S, CHUNK, MIX_TILE, None)
    ck = cache_k[l].reshape(DB * R, KV_WIDTH).astype(F32)
    cv = cache_v[l].reshape(DB * R, KV_WIDTH).astype(F32)
    sc = jnp.pad(state_conv[l].astype(F32), ((0, 0), (CONV_PAD - (CONV_K - 1), 0), (0, 0)))
    ys, ks, vs, us = layer(x_sample.reshape(DB * DS, D_MODEL), DB, DS, DS, DS,
                           (ck, cv, sc.reshape(DB * CONV_PAD * CONV_GROUPS, LANES)))

    RP = min(WINDOW, S)
    heads = lambda a, n, t: a.reshape(n, t, N_KV_HEADS, HEAD_DIM)
    tail = lambda a, n: a.reshape(B, -1, a.shape[-1])[:, -n * (a.shape[0] // (B * S)):]
    new_k_prompt = heads(tail(kp, RP), B, RP)[None]
    new_v_prompt = heads(tail(vp, RP), B, RP)[None]
    new_conv_prompt = tail(up, CONV_K - 1).reshape(B, CONV_K - 1, CONV_WIDTH)[None]
    new_k_sample = jnp.concatenate([cache_k[l].astype(F32), heads(ks, DB, DS)], axis=1)[:, DS:][None]
    new_v_sample = jnp.concatenate([cache_v[l].astype(F32), heads(vs, DB, DS)], axis=1)[:, DS:][None]
    new_conv_sample = jnp.concatenate([state_conv[l].astype(F32), us.reshape(DB, DS, CONV_WIDTH)],
                                      axis=1)[:, DS:][None]
    return (yp.reshape(B, S, D_MODEL), ys.reshape(DB, DS, D_MODEL), new_k_prompt, new_v_prompt,
            new_conv_prompt, new_k_sample, new_v_sample, new_conv_sample)
```

```python
import functools
import itertools
import math

import numpy as np
import jax
import jax.numpy as jnp
from jax import lax
from jax.experimental import pallas as pl
from jax.experimental.pallas import tpu as pltpu

D_MODEL = 1024
CHUNK = 64
HEAD_DIM = 64
N_HEADS = 8
N_KV_HEADS = 2
GROUP = N_HEADS // N_KV_HEADS
ATT_WIDTH = N_HEADS * HEAD_DIM
KV_WIDTH = N_KV_HEADS * HEAD_DIM
CONV_WIDTH = D_MODEL - ATT_WIDTH
CONV_K = 31
WINDOW = 128
QK_WIDTH = ATT_WIDTH + KV_WIDTH
D_FF = 2816
NUM_BUCKETS = 32
MAX_DISTANCE = 128
EPS = 1e-6

LANES = 128
CONV_PAD = 32
FF_SPLITS = (0, 1536, D_FF)
SUBLANES = 8
MXU_TILE = 256
NORM_GROUP = MXU_TILE
CONV_ROWS = 16
CONV_GROUPS = CONV_WIDTH // LANES
TOKEN_TILE = 1024
MIX_TILE = 512
ATT_PIECE = 64
BLOCKS_PER_PIECE = 1
VMEM_LIMIT = 56 * 1024 * 1024

F32 = jnp.float32
BF16 = jnp.bfloat16


def _dot(a, b):
    return jnp.dot(a, b, preferred_element_type=F32)


def _rms(x, g):
    ms = jnp.mean(x * x, axis=-1, keepdims=True)
    return x * lax.rsqrt(ms + EPS) * g


def _inv_rms(x):
    return lax.rsqrt(jnp.mean(x * x, axis=-1, keepdims=True) + EPS)


def _swiglu(xg, inv_rms, wgu_ref, wd_ref):
    acc = None
    for lo, hi in zip(FF_SPLITS[:-1], FF_SPLITS[1:]):
        gate = _dot(xg, wgu_ref[:, lo:hi]) * inv_rms
        up = _dot(xg, wgu_ref[:, D_FF + lo:D_FF + hi]) * inv_rms
        act = (gate * jax.nn.sigmoid(gate) * up).astype(BF16)
        part = _dot(act, wd_ref[lo:hi, :])
        acc = part if acc is None else acc + part
    return acc


def _ffn1_proj_kernel(x_ref, g1_ref, wgu_ref, wd_ref, gmix_ref, win_ref, gsum_ref, gqk_ref,
                      x1_ref, q_ref, k_ref, v_ref, u_ref):
    x = x_ref[...]
    x1 = x + 0.5 * _swiglu((x * g1_ref[...]).astype(BF16), _inv_rms(x), wgu_ref, wd_ref)
    x1_ref[...] = x1
    z = _dot((x1 * gmix_ref[...]).astype(BF16), win_ref[...]) * _inv_rms(x1)
    qk = z[:, :QK_WIDTH]
    sq = (qk * qk).astype(BF16)
    ss = jnp.concatenate(
        [_dot(sq[:, lo:lo + NORM_GROUP], gsum_ref[...]) for lo in range(0, ATT_WIDTH, NORM_GROUP)]
        + [_dot(sq[:, ATT_WIDTH:], gsum_ref[:KV_WIDTH, :KV_WIDTH])], axis=1)
    qkn = qk * lax.rsqrt(ss * (1.0 / HEAD_DIM) + EPS) * gqk_ref[...]
    q_ref[...] = qkn[:, :ATT_WIDTH].astype(BF16)
    k_ref[...] = qkn[:, ATT_WIDTH:]
    v_ref[...] = z[:, QK_WIDTH:QK_WIDTH + KV_WIDTH]
    a = z[:, QK_WIDTH + KV_WIDTH:QK_WIDTH + KV_WIDTH + CONV_WIDTH]
    g = z[:, QK_WIDTH + KV_WIDTH + CONV_WIDTH:]
    u = a * jax.nn.sigmoid(g)
    for grp in range(CONV_GROUPS):
        u_ref[pl.ds(grp, u.shape[0], stride=CONV_GROUPS), :] = u[:, grp * LANES:(grp + 1) * LANES]


def _out_ffn2_kernel(x1_ref, mix_ref, wout_ref, g2_ref, wgu_ref, wd_ref, gfin_ref, y_ref):
    x2 = x1_ref[...] + _dot(mix_ref[...], wout_ref[...])
    x3 = x2 + 0.5 * _swiglu((x2 * g2_ref[...]).astype(BF16), _inv_rms(x2), wgu_ref, wd_ref)
    y_ref[...] = _rms(x3, gfin_ref[...])


def _zero_of(x):
    bits = pltpu.bitcast(x, jnp.uint32)
    half = jnp.uint32(16)
    return lax.shift_right_logical(lax.shift_right_logical(bits, half), half)


def _after(x, zeros):
    packed = 2 * SUBLANES
    top = pltpu.bitcast(x[0:packed, :], jnp.uint32)
    top = top | jnp.concatenate([zeros] * (x.shape[1] // LANES), axis=1)
    return jnp.concatenate([pltpu.bitcast(top, x.dtype), x[packed:, :]], axis=0)


def _mixer_kernel(*refs, chunk, tile):
    stage, attend, conv = _mixer_parts(pl.program_id(1), *refs, chunk=chunk, piece=chunk, tile=tile,
                                       first_tile_has_no_past=False)
    stage()
    scores, values = attend
    for p in range(tile // chunk):
        values(p, scores(p)[0])
    shift, conv_block, n_blocks = conv
    shift()
    for r in range(n_blocks):
        conv_block(r)


def _mixer_ffn2_kernel(q_ref, kc_ref, kp_ref, vc_ref, vp_ref, uc_ref, up_ref, bias_ref, sink_ref,
                       cw_ref, cb_ref, lng_ref, lnb_ref,
                       x1_ref, wout_ref, g2_ref, wgu_ref, wd_ref, gfin_ref,
                       y_ref, mix_ref, x2_ref, xn_ref, act_ref, kw_ref, vw_ref, ue_ref, us_ref, ys_ref,
                       *, chunk, piece, tile, tiles_per_stream):
    s = pl.program_id(0)

    @pl.when(s == 0)
    def _():
        mix_ref[...] = jnp.zeros_like(mix_ref)

    t = jnp.minimum(s, pl.num_programs(0) - 2)
    stage, attend, conv = _mixer_parts(
        lax.rem(t, tiles_per_stream), q_ref, kc_ref, kp_ref, vc_ref, vp_ref, uc_ref, up_ref, bias_ref,
        sink_ref, cw_ref, cb_ref, lng_ref, lnb_ref, mix_ref, kw_ref, vw_ref, ue_ref, us_ref, ys_ref,
        chunk=chunk, piece=piece, tile=tile, first_tile_has_no_past=True)
    cols = lambda n: slice(n * MXU_TILE, (n + 1) * MXU_TILE)

    n_units, n_out = D_FF // MXU_TILE, D_MODEL // MXU_TILE

    def gate_up(n):
        lo = n * MXU_TILE
        gate = _dot(xn_ref[...], wgu_ref[:, lo:lo + MXU_TILE])
        up = _dot(xn_ref[...], wgu_ref[:, D_FF + lo:D_FF + lo + MXU_TILE])
        act_ref[:, cols(n)] = (gate * jax.nn.sigmoid(gate) * up).astype(BF16)

    def down(n):
        x2_ref[:, cols(n)] = x2_ref[:, cols(n)] + 0.5 * _dot(act_ref[...], wd_ref[:, cols(n)])

    for n in range(n_out):
        x2_ref[:, cols(n)] = x1_ref[:, cols(n)] + _dot(mix_ref[...], wout_ref[:, cols(n)])
    stage()
    xn_ref[...] = _rms(x2_ref[...], g2_ref[...]).astype(BF16)
    scores, values = attend
    units = ([functools.partial(gate_up, n) for n in range(n_units)]
             + [functools.partial(down, n) for n in range(n_out)])
    shift, conv_block, n_blocks = conv
    shift()
    hosts = list(range(len(units) - 1)) + list(range(n_units, len(units) - 1))
    n_pieces = tile // piece
    assert n_pieces <= len(hosts)
    starts = [hosts[:n_pieces].count(n) for n in range(len(units))]
    pieces, blocks = iter(range(n_pieces)), iter(range(n_blocks))
    pending = []
    for n, unit in enumerate(units):
        unit()
        for p, prob in pending:
            values(p, prob)
        pending = []
        for p in itertools.islice(pieces, starts[n]):
            prob, zeros = scores(p)
            for r in itertools.islice(blocks, BLOCKS_PER_PIECE):
                zeros = conv_block(r, zeros)
            pending.append((p, _after(prob, zeros)))
    assert not pending
    y_ref[...] = _rms(x2_ref[...], gfin_ref[...])
    for r in blocks:
        conv_block(r)


def _mixer_parts(i, q_ref, kc_ref, kp_ref, vc_ref, vp_ref, uc_ref, up_ref, bias_ref, sink_ref,
                 cw_ref, cb_ref, lng_ref, lnb_ref, mix_ref, kw_ref, vw_ref, ue_ref, us_ref, ys_ref,
                 *, chunk, piece, tile, first_tile_has_no_past):
    kwin_len = WINDOW + chunk
    n_bias = bias_ref.shape[0]
    per_chunk = chunk // piece

    def scores(p):
        c, part = divmod(p, per_chunk)
        low_half = lax.broadcasted_iota(jnp.int32, (piece, LANES), 1) < HEAD_DIM
        sink = sink_ref[...]
        qc = q_ref[p * piece:(p + 1) * piece, :]
        zero = jnp.zeros_like(qc[:, :LANES])
        blocks = []
        for j in range(N_KV_HEADS):
            keep = low_half if j == 0 else jnp.logical_not(low_half)
            for g in range(GROUP):
                blocks.append(jnp.where(keep, qc[:, g * LANES:(g + 1) * LANES], zero))
        q_big = jnp.concatenate(blocks, axis=0)
        kwin = kw_ref[c * chunk:c * chunk + kwin_len, :]
        s = lax.dot_general(kwin, q_big, (((1,), (1,)), ((), ())), preferred_element_type=F32)
        bidx = jnp.minimum(i * (tile // chunk) + c, n_bias - 1)
        sb = s + bias_ref[bidx, part]
        m = jnp.maximum(jnp.max(sb, axis=0, keepdims=True), sink)
        e = jnp.exp(sb - m)
        denom = jnp.sum(e, axis=0, keepdims=True) + jnp.exp(sink - m)
        return (e * (1.0 / denom)).astype(BF16), _zero_of(s[0:SUBLANES, 0:LANES])

    def values(p, prob):
        c = p // per_chunk
        low_half = lax.broadcasted_iota(jnp.int32, (piece, LANES), 1) < HEAD_DIM
        vwin = vw_ref[c * chunk:c * chunk + kwin_len, :]
        o = lax.dot_general(prob, vwin, (((0,), (0,)), ((), ())), preferred_element_type=F32)
        for g in range(GROUP):
            o0 = o[g * piece:(g + 1) * piece, :]
            o1 = o[(GROUP + g) * piece:(GROUP + g + 1) * piece, :]
            mix_ref[p * piece:(p + 1) * piece, g * LANES:(g + 1) * LANES] = (
                jnp.where(low_half, o0, o1).astype(BF16))

    def stage():
        kw_ref[0:WINDOW, :] = kp_ref[...].astype(BF16)
        kw_ref[WINDOW:, :] = kc_ref[...].astype(BF16)
        vw_ref[0:WINDOW, :] = vp_ref[...].astype(BF16)
        vw_ref[WINDOW:, :] = vc_ref[...].astype(BF16)
        u_prev = up_ref[...]
        if first_tile_has_no_past:
            u_prev = jnp.where(i > 0, u_prev, 0.0)
        past_rows = CONV_GROUPS * CONV_PAD
        ue_ref[0:past_rows, :] = u_prev
        ue_ref[past_rows:, :] = uc_ref[...]

    def shift():
        us_ref[...] = ue_ref[CONV_GROUPS:CONV_GROUPS + us_ref.shape[0], :]

    rows = min(CONV_ROWS, tile)
    block_rows = CONV_GROUPS * rows
    first_tap = CONV_PAD - (CONV_K - 1)

    def conv_block(r, start=None):
        if start is None:
            acc = jnp.zeros((block_rows, LANES), F32)
        else:
            acc = pltpu.bitcast(jnp.concatenate([start] * (block_rows // SUBLANES), axis=0), F32)
        for j in range(CONV_K):
            lo = r * block_rows + CONV_GROUPS * (first_tap + j)
            if lo % SUBLANES == 0:
                taps = ue_ref[lo:lo + block_rows, :]
            else:
                taps = us_ref[lo - CONV_GROUPS:lo - CONV_GROUPS + block_rows, :]
            w = cw_ref[j * SUBLANES:(j + 1) * SUBLANES, :]
            acc = acc + taps * jnp.concatenate([w] * (block_rows // SUBLANES), axis=0)
        ys_ref[...] = acc + jnp.concatenate([cb_ref[...]] * (block_rows // SUBLANES), axis=0)
        y = jnp.concatenate([ys_ref[pl.ds(grp, rows, stride=CONV_GROUPS), :]
                             for grp in range(CONV_GROUPS)], axis=1)
        mu = jnp.mean(y, axis=-1, keepdims=True)
        yc = y - mu
        var = jnp.mean(yc * yc, axis=-1, keepdims=True)
        yn = yc * lax.rsqrt(var + EPS) * lng_ref[...] + lnb_ref[...]
        out = (yn * jax.nn.sigmoid(yn)).astype(BF16)
        mix_ref[r * rows:(r + 1) * rows, ATT_WIDTH:] = out
        packed = 2 * SUBLANES
        zeros = _zero_of(out[0:packed, 0:LANES])
        for lo in range(packed, rows, packed):
            zeros = zeros | _zero_of(out[lo:lo + packed, 0:LANES])
        return zeros

    return stage, (scores, values), (shift, conv_block, tile // rows)


def _const_spec(shape):
    return pl.BlockSpec(shape, lambda *_: (0,) * len(shape), pipeline_mode=pl.Buffered(1))


def _params(n_axes):
    return pltpu.CompilerParams(dimension_semantics=("arbitrary",) * n_axes,
                                vmem_limit_bytes=VMEM_LIMIT)


def _ffn1_proj(x, g1, wgu, wd, gmix, win, gsum, gqk):
    n = x.shape[0]
    tm = min(TOKEN_TILE, n)
    row = lambda w: pl.BlockSpec((tm, w), lambda i: (i, 0))
    return pl.pallas_call(
        _ffn1_proj_kernel,
        grid=(n // tm,),
        in_specs=[row(D_MODEL), _const_spec(g1.shape), _const_spec(wgu.shape), _const_spec(wd.shape),
                  _const_spec(gmix.shape), _const_spec(win.shape), _const_spec(gsum.shape),
                  _const_spec(gqk.shape)],
        out_specs=[row(D_MODEL), row(ATT_WIDTH), row(KV_WIDTH), row(KV_WIDTH),
                   pl.BlockSpec((CONV_GROUPS * tm, LANES), lambda i: (i, 0))],
        out_shape=[jax.ShapeDtypeStruct((n, D_MODEL), F32), jax.ShapeDtypeStruct((n, ATT_WIDTH), BF16),
                   jax.ShapeDtypeStruct((n, KV_WIDTH), F32), jax.ShapeDtypeStruct((n, KV_WIDTH), F32),
                   jax.ShapeDtypeStruct((CONV_GROUPS * n, LANES), F32)],
        compiler_params=_params(1),
        name="ffn1_proj",
    )(x, g1, wgu, wd, gmix, win, gsum, gqk)


def _out_ffn2(x1, mix, wout, g2, wgu, wd, gfin):
    n = x1.shape[0]
    tm = min(TOKEN_TILE, n)
    row = lambda w: pl.BlockSpec((tm, w), lambda i: (i, 0))
    return pl.pallas_call(
        _out_ffn2_kernel,
        grid=(n // tm,),
        in_specs=[row(D_MODEL), row(D_MODEL), _const_spec(wout.shape), _const_spec(g2.shape),
                  _const_spec(wgu.shape), _const_spec(wd.shape), _const_spec(gfin.shape)],
        out_specs=row(D_MODEL),
        out_shape=jax.ShapeDtypeStruct((n, D_MODEL), F32),
        compiler_params=_params(1),
        name="out_ffn2",
    )(x1, mix, wout, g2, wgu, wd, gfin)


def _mixer_scratch(tile):
    return [pltpu.VMEM((WINDOW + tile, KV_WIDTH), BF16),
            pltpu.VMEM((WINDOW + tile, KV_WIDTH), BF16),
            pltpu.VMEM((CONV_GROUPS * (CONV_PAD + tile), LANES), F32),
            pltpu.VMEM((CONV_GROUPS * (CONV_PAD + tile) - SUBLANES, LANES), F32),
            pltpu.VMEM((CONV_GROUPS * min(CONV_ROWS, tile), LANES), F32)]


def _mixer(q, k, v, u, k_past, v_past, u_past, bias, sink_row, cw, cb, lng, lnb,
           *, n_streams, seq, chunk, tile):
    nt = seq // tile
    cur = lambda w: pl.BlockSpec((tile, w), lambda b, i: (b * nt + i, 0))
    kv_past = pl.BlockSpec((WINDOW, KV_WIDTH), lambda b, i: (b, 0))
    u_past_spec = pl.BlockSpec((CONV_GROUPS * CONV_PAD, LANES), lambda b, i: (b, 0))
    body = functools.partial(_mixer_kernel, chunk=chunk, tile=tile)
    return pl.pallas_call(
        body,
        grid=(n_streams, nt),
        in_specs=[cur(ATT_WIDTH), cur(KV_WIDTH), kv_past, cur(KV_WIDTH), kv_past,
                  pl.BlockSpec((CONV_GROUPS * tile, LANES), lambda b, i: (b * nt + i, 0)),
                  u_past_spec, _const_spec(bias.shape), _const_spec(sink_row.shape),
                  _const_spec(cw.shape), _const_spec(cb.shape), _const_spec(lng.shape),
                  _const_spec(lnb.shape)],
        out_specs=cur(D_MODEL),
        out_shape=jax.ShapeDtypeStruct((n_streams * seq, D_MODEL), BF16),
        scratch_shapes=_mixer_scratch(tile),
        compiler_params=_params(2),
        name="mixer",
    )(q, k, k_past, v, v_past, u, u_past, bias, sink_row, cw, cb, lng, lnb)


def _mixer_ffn2(q, k, v, u, bias, sink_row, cw, cb, lng, lnb, x1, wout, g2, wgu, wd, gfin,
                *, seq, chunk, tile):
    n_tiles = q.shape[0] // tile
    mix_tile = lambda s: jnp.minimum(s, n_tiles - 1)
    ffn_tile = lambda s: jnp.maximum(s - 1, 0)
    cur = lambda w: pl.BlockSpec((tile, w), lambda s: (mix_tile(s), 0))
    kv_past = pl.BlockSpec(
        (WINDOW, KV_WIDTH), lambda s: (jnp.maximum(mix_tile(s) * (tile // WINDOW) - 1, 0), 0))
    u_past = pl.BlockSpec(
        (CONV_GROUPS * CONV_PAD, LANES),
        lambda s: (jnp.maximum(mix_tile(s) * (tile // CONV_PAD) - 1, 0), 0))
    ffn_row = pl.BlockSpec((tile, D_MODEL), lambda s: (ffn_tile(s), 0))
    body = functools.partial(_mixer_ffn2_kernel, chunk=chunk, piece=ATT_PIECE, tile=tile,
                             tiles_per_stream=seq // tile)
    return pl.pallas_call(
        body,
        grid=(n_tiles + 1,),
        in_specs=[cur(ATT_WIDTH), cur(KV_WIDTH), kv_past, cur(KV_WIDTH), kv_past,
                  pl.BlockSpec((CONV_GROUPS * tile, LANES), lambda s: (mix_tile(s), 0)),
                  u_past, _const_spec(bias.shape), _const_spec(sink_row.shape),
                  _const_spec(cw.shape), _const_spec(cb.shape), _const_spec(lng.shape),
                  _const_spec(lnb.shape),
                  ffn_row, _const_spec(wout.shape), _const_spec(g2.shape), _const_spec(wgu.shape),
                  _const_spec(wd.shape), _const_spec(gfin.shape)],
        out_specs=ffn_row,
        out_shape=jax.ShapeDtypeStruct((q.shape[0], D_MODEL), F32),
        scratch_shapes=[pltpu.VMEM((tile, D_MODEL), BF16), pltpu.VMEM((tile, D_MODEL), F32),
                        pltpu.VMEM((tile, D_MODEL), BF16), pltpu.VMEM((tile, D_FF), BF16)]
        + _mixer_scratch(tile),
        compiler_params=_params(1),
        name="mixer_ffn2",
    )(q, k, k, v, v, u, u, bias, sink_row, cw, cb, lng, lnb, x1, wout, g2, wgu, wd, gfin)


def _rel_bucket(rel):
    nb = NUM_BUCKETS // 2
    max_exact = nb // 2
    ret = np.where(rel > 0, nb, 0)
    n = np.abs(rel)
    nf = np.maximum(n, 1).astype(np.float32)
    large = max_exact + (np.log(nf / np.float32(max_exact)) / np.float32(math.log(MAX_DISTANCE / max_exact))
                         * np.float32(nb - max_exact)).astype(np.int32)
    large = np.minimum(large, nb - 1)
    return (ret + np.where(n < max_exact, n, large)).astype(np.int32)


def _bias_cols(table, qpos, kpos, piece):
    bucket = _rel_bucket(kpos[:, None] - qpos[None, :])
    onehot = jnp.asarray(bucket[:, :, None] == np.arange(NUM_BUCKETS), dtype=F32)
    b = jnp.einsum("kqb,bh->khq", onehot, table.astype(F32), precision=lax.Precision.HIGHEST)
    b = b.reshape(kpos.shape[0], N_HEADS, qpos.shape[0] // piece, piece)
    return jnp.transpose(b, (2, 0, 1, 3)).reshape(-1, kpos.shape[0], N_HEADS * piece)


def kernel(x_prompt, x_sample, cache_k, cache_v, state_conv, rel_bias_table, ffn1_norm, ffn1_w_gu,
           ffn1_w_down, mix_norm, w_in, q_norm, k_norm, sinks, conv_w, conv_b, conv_ln_g, conv_ln_b,
           w_out, ffn2_norm, ffn2_w_gu, ffn2_w_down, final_norm):
    B, S, _ = x_prompt.shape
    DB, DS, _ = x_sample.shape
    R = cache_k.shape[2]
    past_len = R
    assert ffn1_norm.shape[0] == 1 and R == WINDOW and S % MIX_TILE == 0 and DS % 8 == 0
    l = 0
    row = lambda a: a.reshape(1, -1).astype(F32)

    pair = lambda a, axis: jnp.swapaxes(
        a.reshape(a.shape[:axis] + (N_KV_HEADS, GROUP, HEAD_DIM) + a.shape[axis + 1:]),
        axis, axis + 1).reshape(a.shape)
    win = jnp.concatenate([pair(w_in[l][:, :ATT_WIDTH], 1), w_in[l][:, ATT_WIDTH:]], axis=1).astype(BF16)
    wout = jnp.concatenate([pair(w_out[l][:ATT_WIDTH], 0), w_out[l][ATT_WIDTH:]], axis=0).astype(BF16)
    gqk = jnp.concatenate([jnp.tile(q_norm[l], N_HEADS) * (HEAD_DIM ** -0.5),
                           jnp.tile(k_norm[l], N_KV_HEADS)]).reshape(1, QK_WIDTH).astype(F32)
    head_of_col = np.arange(NORM_GROUP) // HEAD_DIM
    gsum = jnp.asarray(head_of_col[:, None] == head_of_col[None, :], dtype=BF16)
    wgu1, wd1 = ffn1_w_gu[l].astype(BF16), ffn1_w_down[l].astype(BF16)
    wgu2, wd2 = ffn2_w_gu[l].astype(BF16), ffn2_w_down[l].astype(BF16)
    by_group = lambda a: jnp.tile(a.astype(F32).reshape(-1, CONV_GROUPS, LANES),
                                  (1, SUBLANES // CONV_GROUPS, 1)).reshape(-1, LANES)
    cw, cb = by_group(conv_w[l]), by_group(conv_b[l])
    lng, lnb = row(conv_ln_g[l]), row(conv_ln_b[l])

    band = WINDOW + CHUNK
    bias_p = _bias_cols(rel_bias_table, np.arange(CHUNK), np.arange(band) - WINDOW, ATT_PIECE)
    key = np.arange(band)[None, :, None]
    neg = lambda first_valid: jnp.where(key >= first_valid, bias_p, -jnp.inf)
    bias_prompt = jnp.stack([neg(WINDOW), neg(WINDOW - CHUNK), bias_p])
    qpos_s = past_len + np.arange(DS)
    kpos_s = np.concatenate([past_len - R + np.arange(R), qpos_s])
    bias_sample = _bias_cols(rel_bias_table, qpos_s, kpos_s, DS)[None]
    sink_row = lambda n: jnp.broadcast_to(sinks[l].astype(F32)[:, None], (N_HEADS, n)).reshape(1, N_HEADS * n)

    def layer(x, n_streams, seq, chunk, tile, past):
        x1, q, k, v, u = _ffn1_proj(x, row(ffn1_norm[l]), wgu1, wd1, row(mix_norm[l]), win, gsum, gqk)
        if past is None:
            y = _mixer_ffn2(q, k, v, u, bias_prompt, sink_row(ATT_PIECE), cw, cb, lng, lnb, x1, wout,
                            row(ffn2_norm[l]), wgu2, wd2, row(final_norm[l]),
                            seq=seq, chunk=chunk, tile=tile)
        else:
            mix = _mixer(q, k, v, u, *past, bias_sample, sink_row(chunk), cw, cb, lng, lnb,
                         n_streams=n_streams, seq=seq, chunk=chunk, tile=tile)
            y = _out_ffn2(x1, mix, wout, row(ffn2_norm[l]), wgu2, wd2, row(final_norm[l]))
        return y, k, v, u

    yp, kp, vp, up = layer(x_prompt.reshape(B * S, D_MODEL), B, S, CHUNK, MIX_TILE, None)
    ck = cache_k[l].reshape(DB * R, KV_WIDTH).astype(F32)
    cv = cache_v[l].reshape(DB * R, KV_WIDTH).astype(F32)
    sc = jnp.pad(state_conv[l].astype(F32), ((0, 0), (CONV_PAD - (CONV_K - 1), 0), (0, 0)))
    ys, ks, vs, us = layer(x_sample.reshape(DB * DS, D_MODEL), DB, DS, DS, DS,
                           (ck, cv, sc.reshape(DB * CONV_PAD * CONV_GROUPS, LANES)))

    RP = min(WINDOW, S)
    heads = lambda a, n, t: a.reshape(n, t, N_KV_HEADS, HEAD_DIM)
    tail = lambda a, n: a.reshape(B, -1, a.shape[-1])[:, -n * (a.shape[0] // (B * S)):]
    new_k_prompt = heads(tail(kp, RP), B, RP)[None]
    new_v_prompt = heads(tail(vp, RP), B, RP)[None]
    new_conv_prompt = tail(up, CONV_K - 1).reshape(B, CONV_K - 1, CONV_WIDTH)[None]
    new_k_sample = jnp.concatenate([cache_k[l].astype(F32), heads(ks, DB, DS)], axis=1)[:, DS:][None]
    new_v_sample = jnp.concatenate([cache_v[l].astype(F32), heads(vs, DB, DS)], axis=1)[:, DS:][None]
    new_conv_sample = jnp.concatenate([state_conv[l].astype(F32), us.reshape(DB, DS, CONV_WIDTH)],
                                      axis=1)[:, DS:][None]
    return (yp.reshape(B, S, D_MODEL), ys.reshape(DB, DS, D_MODEL), new_k_prompt, new_v_prompt,
            new_conv_prompt, new_k_sample, new_v_sample, new_conv_sample)
```

```python
import functools
import itertools
import math

import numpy as np
import jax
import jax.numpy as jnp
from jax import lax
from jax.experimental import pallas as pl
from jax.experimental.pallas import tpu as pltpu

D_MODEL = 1024
CHUNK = 64
HEAD_DIM = 64
N_HEADS = 8
N_KV_HEADS = 2
GROUP = N_HEADS // N_KV_HEADS
ATT_WIDTH = N_HEADS * HEAD_DIM
KV_WIDTH = N_KV_HEADS * HEAD_DIM
CONV_WIDTH = D_MODEL - ATT_WIDTH
CONV_K = 31
WINDOW = 128
IN_WIDTH = ATT_WIDTH + 2 * KV_WIDTH + 2 * CONV_WIDTH
QK_WIDTH = ATT_WIDTH + KV_WIDTH
D_FF = 2816
NUM_BUCKETS = 32
MAX_DISTANCE = 128
EPS = 1e-6

LANES = 128
CONV_PAD = 32
FF_SPLITS = (0, 1536, D_FF)
SUBLANES = 8
MXU_TILE = 256
NORM_GROUP = MXU_TILE
CONV_ROWS = 16
CONV_GROUPS = CONV_WIDTH // LANES
TOKEN_TILE = 1024
MIX_TILE = 512
ATT_PIECE = 64
BLOCKS_PER_PIECE = 1
VMEM_LIMIT = 56 * 1024 * 1024

F32 = jnp.float32
BF16 = jnp.bfloat16


def _dot(a, b):
    return jnp.dot(a, b, preferred_element_type=F32)


def _rms(x, g):
    ms = jnp.mean(x * x, axis=-1, keepdims=True)
    return x * lax.rsqrt(ms + EPS) * g


def _inv_rms(x):
    return lax.rsqrt(jnp.mean(x * x, axis=-1, keepdims=True) + EPS)


def _swiglu(xn, wgu_ref, wd_ref, row_scale=None):
    acc = None
    for lo, hi in zip(FF_SPLITS[:-1], FF_SPLITS[1:]):
        gate = _dot(xn, wgu_ref[:, lo:hi])
        up = _dot(xn, wgu_ref[:, D_FF + lo:D_FF + hi])
        if row_scale is not None:
            gate, up = gate * row_scale, up * row_scale
        act = (gate * jax.nn.sigmoid(gate) * up).astype(BF16)
        part = _dot(act, wd_ref[lo:hi, :])
        acc = part if acc is None else acc + part
    return acc


def _ffn1_proj_kernel(x_ref, g1_ref, wgu_ref, wd_ref, gmix_ref, win_ref, gsum_ref, gqk_ref,
                      x1_ref, q_ref, k_ref, v_ref, u_ref):
    x = x_ref[...]
    x1 = x + 0.5 * _swiglu((x * g1_ref[...]).astype(BF16), wgu_ref, wd_ref, _inv_rms(x))
    x1_ref[...] = x1
    z = _dot((x1 * gmix_ref[...]).astype(BF16), win_ref[...]) * _inv_rms(x1)
    qk = z[:, :QK_WIDTH]
    sq = (qk * qk).astype(BF16)
    ss = jnp.concatenate(
        [_dot(sq[:, lo:lo + NORM_GROUP], gsum_ref[...]) for lo in range(0, ATT_WIDTH, NORM_GROUP)]
        + [_dot(sq[:, ATT_WIDTH:], gsum_ref[:KV_WIDTH, :KV_WIDTH])], axis=1)
    qkn = qk * lax.rsqrt(ss * (1.0 / HEAD_DIM) + EPS) * gqk_ref[...]
    q_ref[...] = qkn[:, :ATT_WIDTH].astype(BF16)
    k_ref[...] = qkn[:, ATT_WIDTH:]
    v_ref[...] = z[:, QK_WIDTH:QK_WIDTH + KV_WIDTH]
    a = z[:, QK_WIDTH + KV_WIDTH:QK_WIDTH + KV_WIDTH + CONV_WIDTH]
    g = z[:, QK_WIDTH + KV_WIDTH + CONV_WIDTH:]
    u = a * jax.nn.sigmoid(g)
    for grp in range(CONV_GROUPS):
        u_ref[pl.ds(grp, u.shape[0], stride=CONV_GROUPS), :] = u[:, grp * LANES:(grp + 1) * LANES]


def _out_ffn2_kernel(x1_ref, mix_ref, wout_ref, g2_ref, wgu_ref, wd_ref, gfin_ref, y_ref):
    x2 = x1_ref[...] + _dot(mix_ref[...], wout_ref[...])
    x3 = x2 + 0.5 * _swiglu((x2 * g2_ref[...]).astype(BF16), wgu_ref, wd_ref, _inv_rms(x2))
    y_ref[...] = _rms(x3, gfin_ref[...])


def _zero_of(x):
    bits = pltpu.bitcast(x, jnp.uint32)
    half = jnp.uint32(16)
    return lax.shift_right_logical(lax.shift_right_logical(bits, half), half)


def _after(x, zeros):
    packed = 2 * SUBLANES
    top = pltpu.bitcast(x[0:packed, :], jnp.uint32)
    top = top | jnp.concatenate([zeros] * (x.shape[1] // LANES), axis=1)
    return jnp.concatenate([pltpu.bitcast(top, x.dtype), x[packed:, :]], axis=0)


def _mixer_kernel(*refs, chunk, tile):
    stage, attend, conv = _mixer_parts(pl.program_id(1), *refs, chunk=chunk, piece=chunk, tile=tile,
                                       first_tile_has_no_past=False)
    stage()
    scores, values = attend
    for p in range(tile // chunk):
        values(p, scores(p)[0])
    shift, conv_block, n_blocks = conv
    shift()
    for r in range(n_blocks):
        conv_block(r)


def _mixer_ffn2_kernel(q_ref, kc_ref, kp_ref, vc_ref, vp_ref, uc_ref, up_ref, bias_ref, sink_ref,
                       cw_ref, cb_ref, lng_ref, lnb_ref,
                       x1_ref, wout_ref, g2_ref, wgu_ref, wd_ref, gfin_ref,
                       y_ref, mix_ref, x2_ref, xn_ref, xg_ref, act_ref, kw_ref, vw_ref, ue_ref, us_ref, ys_ref,
                       *, chunk, piece, tile, tiles_per_stream):
    s = pl.program_id(0)

    @pl.when(s == 0)
    def _():
        mix_ref[...] = jnp.zeros_like(mix_ref)

    t = jnp.minimum(s, pl.num_programs(0) - 2)
    stage, attend, conv = _mixer_parts(
        lax.rem(t, tiles_per_stream), q_ref, kc_ref, kp_ref, vc_ref, vp_ref, uc_ref, up_ref, bias_ref,
        sink_ref, cw_ref, cb_ref, lng_ref, lnb_ref, mix_ref, kw_ref, vw_ref, ue_ref, us_ref, ys_ref,
        chunk=chunk, piece=piece, tile=tile, first_tile_has_no_past=True)
    cols = lambda n: slice(n * MXU_TILE, (n + 1) * MXU_TILE)

    n_units, n_out = D_FF // MXU_TILE, D_MODEL // MXU_TILE

    def gate_up(n):
        lo = n * MXU_TILE
        if n == 0:
            gate = _dot(xg_ref[...], wgu_ref[:, lo:lo + MXU_TILE]) * inv_rms
            up = _dot(xg_ref[...], wgu_ref[:, D_FF + lo:D_FF + lo + MXU_TILE]) * inv_rms
        else:
            gate = _dot(xn_ref[...], wgu_ref[:, lo:lo + MXU_TILE])
            up = _dot(xn_ref[...], wgu_ref[:, D_FF + lo:D_FF + lo + MXU_TILE])
        act_ref[:, cols(n)] = (gate * jax.nn.sigmoid(gate) * up).astype(BF16)

    def down(n):
        x2_ref[:, cols(n)] = x2_ref[:, cols(n)] + 0.5 * _dot(act_ref[...], wd_ref[:, cols(n)])

    for n in range(n_out):
        x2_ref[:, cols(n)] = x1_ref[:, cols(n)] + _dot(mix_ref[...], wout_ref[:, cols(n)])
    stage()
    xg_ref[...] = (x2_ref[...] * g2_ref[...]).astype(BF16)
    inv_rms = _inv_rms(x2_ref[...])
    xn_ref[...] = (x2_ref[...] * inv_rms * g2_ref[...]).astype(BF16)
    scores, values = attend
    units = ([functools.partial(gate_up, n) for n in range(n_units)]
             + [functools.partial(down, n) for n in range(n_out)])
    shift, conv_block, n_blocks = conv
    shift()
    hosts = list(range(len(units) - 1)) + list(range(n_units, len(units) - 1))
    n_pieces = tile // piece
    assert n_pieces <= len(hosts)
    starts = [hosts[:n_pieces].count(n) for n in range(len(units))]
    pieces, blocks = iter(range(n_pieces)), iter(range(n_blocks))
    pending = []
    for n, unit in enumerate(units):
        unit()
        for p, prob in pending:
            values(p, prob)
        pending = []
        for p in itertools.islice(pieces, starts[n]):
            prob, zeros = scores(p)
            for r in itertools.islice(blocks, BLOCKS_PER_PIECE):
                zeros = conv_block(r, zeros)
            pending.append((p, _after(prob, zeros)))
    assert not pending
    y_ref[...] = _rms(x2_ref[...], gfin_ref[...])
    for r in blocks:
        conv_block(r)


def _mixer_parts(i, q_ref, kc_ref, kp_ref, vc_ref, vp_ref, uc_ref, up_ref, bias_ref, sink_ref,
                 cw_ref, cb_ref, lng_ref, lnb_ref, mix_ref, kw_ref, vw_ref, ue_ref, us_ref, ys_ref,
                 *, chunk, piece, tile, first_tile_has_no_past):
    kwin_len = WINDOW + chunk
    n_bias = bias_ref.shape[0]
    per_chunk = chunk // piece

    def scores(p):
        c, part = divmod(p, per_chunk)
        low_half = lax.broadcasted_iota(jnp.int32, (piece, LANES), 1) < HEAD_DIM
        sink = sink_ref[...]
        qc = q_ref[p * piece:(p + 1) * piece, :]
        zero = jnp.zeros_like(qc[:, :LANES])
        blocks = []
        for j in range(N_KV_HEADS):
            keep = low_half if j == 0 else jnp.logical_not(low_half)
            for g in range(GROUP):
                blocks.append(jnp.where(keep, qc[:, g * LANES:(g + 1) * LANES], zero))
        q_big = jnp.concatenate(blocks, axis=0)
        kwin = kw_ref[c * chunk:c * chunk + kwin_len, :]
        s = lax.dot_general(kwin, q_big, (((1,), (1,)), ((), ())), preferred_element_type=F32)
        bidx = jnp.minimum(i * (tile // chunk) + c, n_bias - 1)
        sb = s + bias_ref[bidx, part]
        m = jnp.maximum(jnp.max(sb, axis=0, keepdims=True), sink)
        e = jnp.exp(sb - m)
        denom = jnp.sum(e, axis=0, keepdims=True) + jnp.exp(sink - m)
        return (e * (1.0 / denom)).astype(BF16), _zero_of(s[0:SUBLANES, 0:LANES])

    def values(p, prob):
        c = p // per_chunk
        low_half = lax.broadcasted_iota(jnp.int32, (piece, LANES), 1) < HEAD_DIM
        vwin = vw_ref[c * chunk:c * chunk + kwin_len, :]
        o = lax.dot_general(prob, vwin, (((0,), (0,)), ((), ())), preferred_element_type=F32)
        for g in range(GROUP):
            o0 = o[g * piece:(g + 1) * piece, :]
            o1 = o[(GROUP + g) * piece:(GROUP + g + 1) * piece, :]
            mix_ref[p * piece:(p + 1) * piece, g * LANES:(g + 1) * LANES] = (
                jnp.where(low_half, o0, o1).astype(BF16))

    def stage():
        kw_ref[0:WINDOW, :] = kp_ref[...].astype(BF16)
        kw_ref[WINDOW:, :] = kc_ref[...].astype(BF16)
        vw_ref[0:WINDOW, :] = vp_ref[...].astype(BF16)
        vw_ref[WINDOW:, :] = vc_ref[...].astype(BF16)
        u_prev = up_ref[...]
        if first_tile_has_no_past:
            u_prev = jnp.where(i > 0, u_prev, 0.0)
        past_rows = CONV_GROUPS * CONV_PAD
        ue_ref[0:past_rows, :] = u_prev
        ue_ref[past_rows:, :] = uc_ref[...]

    def shift():
        us_ref[...] = ue_ref[CONV_GROUPS:CONV_GROUPS + us_ref.shape[0], :]

    rows = min(CONV_ROWS, tile)
    block_rows = CONV_GROUPS * rows
    first_tap = CONV_PAD - (CONV_K - 1)

    def conv_block(r, start=None):
        if start is None:
            acc = jnp.zeros((block_rows, LANES), F32)
        else:
            acc = pltpu.bitcast(jnp.concatenate([start] * (block_rows // SUBLANES), axis=0), F32)
        for j in range(CONV_K):
            lo = r * block_rows + CONV_GROUPS * (first_tap + j)
            if lo % SUBLANES == 0:
                taps = ue_ref[lo:lo + block_rows, :]
            else:
                taps = us_ref[lo - CONV_GROUPS:lo - CONV_GROUPS + block_rows, :]
            w = cw_ref[j * SUBLANES:(j + 1) * SUBLANES, :]
            acc = acc + taps * jnp.concatenate([w] * (block_rows // SUBLANES), axis=0)
        ys_ref[...] = acc + jnp.concatenate([cb_ref[...]] * (block_rows // SUBLANES), axis=0)
        y = jnp.concatenate([ys_ref[pl.ds(grp, rows, stride=CONV_GROUPS), :]
                             for grp in range(CONV_GROUPS)], axis=1)
        mu = jnp.mean(y, axis=-1, keepdims=True)
        yc = y - mu
        var = jnp.mean(yc * yc, axis=-1, keepdims=True)
        yn = yc * lax.rsqrt(var + EPS) * lng_ref[...] + lnb_ref[...]
        out = (yn * jax.nn.sigmoid(yn)).astype(BF16)
        mix_ref[r * rows:(r + 1) * rows, ATT_WIDTH:] = out
        packed = 2 * SUBLANES
        zeros = _zero_of(out[0:packed, 0:LANES])
        for lo in range(packed, rows, packed):
            zeros = zeros | _zero_of(out[lo:lo + packed, 0:LANES])
        return zeros

    return stage, (scores, values), (shift, conv_block, tile // rows)


def _const_spec(shape):
    return pl.BlockSpec(shape, lambda *_: (0,) * len(shape), pipeline_mode=pl.Buffered(1))


def _params(n_axes):
    return pltpu.CompilerParams(dimension_semantics=("arbitrary",) * n_axes,
                                vmem_limit_bytes=VMEM_LIMIT)


def _ffn1_proj(x, g1, wgu, wd, gmix, win, gsum, gqk):
    n = x.shape[0]
    tm = min(TOKEN_TILE, n)
    row = lambda w: pl.BlockSpec((tm, w), lambda i: (i, 0))
    return pl.pallas_call(
        _ffn1_proj_kernel,
        grid=(n // tm,),
        in_specs=[row(D_MODEL), _const_spec(g1.shape), _const_spec(wgu.shape), _const_spec(wd.shape),
                  _const_spec(gmix.shape), _const_spec(win.shape), _const_spec(gsum.shape),
                  _const_spec(gqk.shape)],
        out_specs=[row(D_MODEL), row(ATT_WIDTH), row(KV_WIDTH), row(KV_WIDTH),
                   pl.BlockSpec((CONV_GROUPS * tm, LANES), lambda i: (i, 0))],
        out_shape=[jax.ShapeDtypeStruct((n, D_MODEL), F32), jax.ShapeDtypeStruct((n, ATT_WIDTH), BF16),
                   jax.ShapeDtypeStruct((n, KV_WIDTH), F32), jax.ShapeDtypeStruct((n, KV_WIDTH), F32),
                   jax.ShapeDtypeStruct((CONV_GROUPS * n, LANES), F32)],
        compiler_params=_params(1),
        name="ffn1_proj",
    )(x, g1, wgu, wd, gmix, win, gsum, gqk)


def _out_ffn2(x1, mix, wout, g2, wgu, wd, gfin):
    n = x1.shape[0]
    tm = min(TOKEN_TILE, n)
    row = lambda w: pl.BlockSpec((tm, w), lambda i: (i, 0))
    return pl.pallas_call(
        _out_ffn2_kernel,
        grid=(n // tm,),
        in_specs=[row(D_MODEL), row(D_MODEL), _const_spec(wout.shape), _const_spec(g2.shape),
                  _const_spec(wgu.shape), _const_spec(wd.shape), _const_spec(gfin.shape)],
        out_specs=row(D_MODEL),
        out_shape=jax.ShapeDtypeStruct((n, D_MODEL), F32),
        compiler_params=_params(1),
        name="out_ffn2",
    )(x1, mix, wout, g2, wgu, wd, gfin)


def _mixer_scratch(tile):
    return [pltpu.VMEM((WINDOW + tile, KV_WIDTH), BF16),
            pltpu.VMEM((WINDOW + tile, KV_WIDTH), BF16),
            pltpu.VMEM((CONV_GROUPS * (CONV_PAD + tile), LANES), F32),
            pltpu.VMEM((CONV_GROUPS * (CONV_PAD + tile) - SUBLANES, LANES), F32),
            pltpu.VMEM((CONV_GROUPS * min(CONV_ROWS, tile), LANES), F32)]


def _mixer(q, k, v, u, k_past, v_past, u_past, bias, sink_row, cw, cb, lng, lnb,
           *, n_streams, seq, chunk, tile):
    nt = seq // tile
    cur = lambda w: pl.BlockSpec((tile, w), lambda b, i: (b * nt + i, 0))
    kv_past = pl.BlockSpec((WINDOW, KV_WIDTH), lambda b, i: (b, 0))
    u_past_spec = pl.BlockSpec((CONV_GROUPS * CONV_PAD, LANES), lambda b, i: (b, 0))
    body = functools.partial(_mixer_kernel, chunk=chunk, tile=tile)
    return pl.pallas_call(
        body,
        grid=(n_streams, nt),
        in_specs=[cur(ATT_WIDTH), cur(KV_WIDTH), kv_past, cur(KV_WIDTH), kv_past,
                  pl.BlockSpec((CONV_GROUPS * tile, LANES), lambda b, i: (b * nt + i, 0)),
                  u_past_spec, _const_spec(bias.shape), _const_spec(sink_row.shape),
                  _const_spec(cw.shape), _const_spec(cb.shape), _const_spec(lng.shape),
                  _const_spec(lnb.shape)],
        out_specs=cur(D_MODEL),
        out_shape=jax.ShapeDtypeStruct((n_streams * seq, D_MODEL), BF16),
        scratch_shapes=_mixer_scratch(tile),
        compiler_params=_params(2),
        name="mixer",
    )(q, k, k_past, v, v_past, u, u_past, bias, sink_row, cw, cb, lng, lnb)


def _mixer_ffn2(q, k, v, u, bias, sink_row, cw, cb, lng, lnb, x1, wout, g2, wgu, wd, gfin,
                *, seq, chunk, tile):
    n_tiles = q.shape[0] // tile
    mix_tile = lambda s: jnp.minimum(s, n_tiles - 1)
    ffn_tile = lambda s: jnp.maximum(s - 1, 0)
    cur = lambda w: pl.BlockSpec((tile, w), lambda s: (mix_tile(s), 0))
    kv_past = pl.BlockSpec(
        (WINDOW, KV_WIDTH), lambda s: (jnp.maximum(mix_tile(s) * (tile // WINDOW) - 1, 0), 0))
    u_past = pl.BlockSpec(
        (CONV_GROUPS * CONV_PAD, LANES),
        lambda s: (jnp.maximum(mix_tile(s) * (tile // CONV_PAD) - 1, 0), 0))
    ffn_row = pl.BlockSpec((tile, D_MODEL), lambda s: (ffn_tile(s), 0))
    body = functools.partial(_mixer_ffn2_kernel, chunk=chunk, piece=ATT_PIECE, tile=tile,
                             tiles_per_stream=seq // tile)
    return pl.pallas_call(
        body,
        grid=(n_tiles + 1,),
        in_specs=[cur(ATT_WIDTH), cur(KV_WIDTH), kv_past, cur(KV_WIDTH), kv_past,
                  pl.BlockSpec((CONV_GROUPS * tile, LANES), lambda s: (mix_tile(s), 0)),
                  u_past, _const_spec(bias.shape), _const_spec(sink_row.shape),
                  _const_spec(cw.shape), _const_spec(cb.shape), _const_spec(lng.shape),
                  _const_spec(lnb.shape),
                  ffn_row, _const_spec(wout.shape), _const_spec(g2.shape), _const_spec(wgu.shape),
                  _const_spec(wd.shape), _const_spec(gfin.shape)],
        out_specs=ffn_row,
        out_shape=jax.ShapeDtypeStruct((q.shape[0], D_MODEL), F32),
        scratch_shapes=[pltpu.VMEM((tile, D_MODEL), BF16), pltpu.VMEM((tile, D_MODEL), F32),
                        pltpu.VMEM((tile, D_MODEL), BF16), pltpu.VMEM((tile, D_MODEL), BF16),
                        pltpu.VMEM((tile, D_FF), BF16)]
        + _mixer_scratch(tile),
        compiler_params=_params(1),
        name="mixer_ffn2",
    )(q, k, k, v, v, u, u, bias, sink_row, cw, cb, lng, lnb, x1, wout, g2, wgu, wd, gfin)


def _rel_bucket(rel):
    nb = NUM_BUCKETS // 2
    max_exact = nb // 2
    ret = np.where(rel > 0, nb, 0)
    n = np.abs(rel)
    nf = np.maximum(n, 1).astype(np.float32)
    large = max_exact + (np.log(nf / np.float32(max_exact)) / np.float32(math.log(MAX_DISTANCE / max_exact))
                         * np.float32(nb - max_exact)).astype(np.int32)
    large = np.minimum(large, nb - 1)
    return (ret + np.where(n < max_exact, n, large)).astype(np.int32)


def _bias_cols(table, qpos, kpos, piece):
    bucket = _rel_bucket(kpos[:, None] - qpos[None, :])
    onehot = jnp.asarray(bucket[:, :, None] == np.arange(NUM_BUCKETS), dtype=F32)
    b = jnp.einsum("kqb,bh->khq", onehot, table.astype(F32), precision=lax.Precision.HIGHEST)
    b = b.reshape(kpos.shape[0], N_HEADS, qpos.shape[0] // piece, piece)
    return jnp.transpose(b, (2, 0, 1, 3)).reshape(-1, kpos.shape[0], N_HEADS * piece)


def kernel(x_prompt, x_sample, cache_k, cache_v, state_conv, rel_bias_table, ffn1_norm, ffn1_w_gu,
           ffn1_w_down, mix_norm, w_in, q_norm, k_norm, sinks, conv_w, conv_b, conv_ln_g, conv_ln_b,
           w_out, ffn2_norm, ffn2_w_gu, ffn2_w_down, final_norm):
    B, S, _ = x_prompt.shape
    DB, DS, _ = x_sample.shape
    R = cache_k.shape[2]
    past_len = R
    assert ffn1_norm.shape[0] == 1 and R == WINDOW and S % MIX_TILE == 0 and DS % 8 == 0
    l = 0
    row = lambda a: a.reshape(1, -1).astype(F32)

    pair = lambda a, axis: jnp.swapaxes(
        a.reshape(a.shape[:axis] + (N_KV_HEADS, GROUP, HEAD_DIM) + a.shape[axis + 1:]),
        axis, axis + 1).reshape(a.shape)
    win = jnp.concatenate([pair(w_in[l][:, :ATT_WIDTH], 1), w_in[l][:, ATT_WIDTH:]], axis=1).astype(BF16)
    wout = jnp.concatenate([pair(w_out[l][:ATT_WIDTH], 0), w_out[l][ATT_WIDTH:]], axis=0).astype(BF16)
    gqk = jnp.concatenate([jnp.tile(q_norm[l], N_HEADS) * (HEAD_DIM ** -0.5),
                           jnp.tile(k_norm[l], N_KV_HEADS)]).reshape(1, QK_WIDTH).astype(F32)
    head_of_col = np.arange(NORM_GROUP) // HEAD_DIM
    gsum = jnp.asarray(head_of_col[:, None] == head_of_col[None, :], dtype=BF16)
    wgu1, wd1 = ffn1_w_gu[l].astype(BF16), ffn1_w_down[l].astype(BF16)
    wgu2, wd2 = ffn2_w_gu[l].astype(BF16), ffn2_w_down[l].astype(BF16)
    by_group = lambda a: jnp.tile(a.astype(F32).reshape(-1, CONV_GROUPS, LANES),
                                  (1, SUBLANES // CONV_GROUPS, 1)).reshape(-1, LANES)
    cw, cb = by_group(conv_w[l]), by_group(conv_b[l])
    lng, lnb = row(conv_ln_g[l]), row(conv_ln_b[l])

    band = WINDOW + CHUNK
    bias_p = _bias_cols(rel_bias_table, np.arange(CHUNK), np.arange(band) - WINDOW, ATT_PIECE)
    key = np.arange(band)[None, :, None]
    neg = lambda first_valid: jnp.where(key >= first_valid, bias_p, -jnp.inf)
    bias_prompt = jnp.stack([neg(WINDOW), neg(WINDOW - CHUNK), bias_p])
    qpos_s = past_len + np.arange(DS)
    kpos_s = np.concatenate([past_len - R + np.arange(R), qpos_s])
    bias_sample = _bias_cols(rel_bias_table, qpos_s, kpos_s, DS)[None]
    sink_row = lambda n: jnp.broadcast_to(sinks[l].astype(F32)[:, None], (N_HEADS, n)).reshape(1, N_HEADS * n)

    def layer(x, n_streams, seq, chunk, tile, past):
        x1, q, k, v, u = _ffn1_proj(x, row(ffn1_norm[l]), wgu1, wd1, row(mix_norm[l]), win, gsum, gqk)
        if past is None:
            y = _mixer_ffn2(q, k, v, u, bias_prompt, sink_row(ATT_PIECE), cw, cb, lng, lnb, x1, wout,
                            row(ffn2_norm[l]), wgu2, wd2, row(final_norm[l]),
                            seq=seq, chunk=chunk, tile=tile)
        else:
            mix = _mixer(q, k, v, u, *past, bias_sample, sink_row(chunk), cw, cb, lng, lnb,
                         n_streams=n_streams, seq=seq, chunk=chunk, tile=tile)
            y = _out_ffn2(x1, mix, wout, row(ffn2_norm[l]), wgu2, wd2, row(final_norm[l]))
        return y, k, v, u

    yp, kp, vp, up = layer(x_prompt.reshape(B * S, D_MODEL), B, S, CHUNK, MIX_TILE, None)
    ck = cache_k[l].reshape(DB * R, KV_WIDTH).astype(F32)
    cv = cache_v[l].reshape(DB * R, KV_WIDTH).astype(F32)
    sc = jnp.pad(state_conv[l].astype(F32), ((0, 0), (CONV_PAD - (CONV_K - 1), 0), (0, 0)))
    ys, ks, vs, us = layer(x_sample.reshape(DB * DS, D_MODEL), DB, DS, DS, DS,
                           (ck, cv, sc.reshape(DB * CONV_PAD * CONV_GROUPS, LANES)))

    RP = min(WINDOW, S)
    heads = lambda a, n, t: a.reshape(n, t, N_KV_HEADS, HEAD_DIM)
    tail = lambda a, n: a.reshape(B, -1, a.shape[-1])[:, -n * (a.shape[0] // (B * S)):]
    new_k_prompt = heads(tail(kp, RP), B, RP)[None]
    new_v_prompt = heads(tail(vp, RP), B, RP)[None]
    new_conv_prompt = tail(up, CONV_K - 1).reshape(B, CONV_K - 1, CONV_WIDTH)[None]
    new_k_sample = jnp.concatenate([cache_k[l].astype(F32), heads(ks, DB, DS)], axis=1)[:, DS:][None]
    new_v_sample = jnp.concatenate([cache_v[l].astype(F32), heads(vs, DB, DS)], axis=1)[:, DS:][None]
    new_conv_sample = jnp.concatenate([state_conv[l].astype(F32), us.reshape(DB, DS, CONV_WIDTH)],
                                      axis=1)[:, DS:][None]
    return (yp.reshape(B, S, D_MODEL), ys.reshape(DB, DS, D_MODEL), new_k_prompt, new_v_prompt,
            new_conv_prompt, new_k_sample, new_v_sample, new_conv_sample)
```

```python
import functools
import itertools
import math

import numpy as np
import jax
import jax.numpy as jnp
from jax import lax
from jax.experimental import pallas as pl
from jax.experimental.pallas import tpu as pltpu

D_MODEL = 1024
CHUNK = 64
HEAD_DIM = 64
N_HEADS = 8
N_KV_HEADS = 2
GROUP = N_HEADS // N_KV_HEADS
ATT_WIDTH = N_HEADS * HEAD_DIM
KV_WIDTH = N_KV_HEADS * HEAD_DIM
CONV_WIDTH = D_MODEL - ATT_WIDTH
CONV_K = 31
WINDOW = 128
IN_WIDTH = ATT_WIDTH + 2 * KV_WIDTH + 2 * CONV_WIDTH
QK_WIDTH = ATT_WIDTH + KV_WIDTH
D_FF = 2816
NUM_BUCKETS = 32
MAX_DISTANCE = 128
EPS = 1e-6

LANES = 128
CONV_PAD = 32
FF_SPLITS = (0, 1536, D_FF)
SUBLANES = 8
MXU_TILE = 256
NORM_GROUP = MXU_TILE
CONV_ROWS = 16
CONV_GROUPS = CONV_WIDTH // LANES
TOKEN_TILE = 1024
MIX_TILE = 512
ATT_PIECE = 64
BLOCKS_PER_PIECE = 1
VMEM_LIMIT = 56 * 1024 * 1024

F32 = jnp.float32
BF16 = jnp.bfloat16


def _dot(a, b):
    return jnp.dot(a, b, preferred_element_type=F32)


def _rms(x, g):
    ms = jnp.mean(x * x, axis=-1, keepdims=True)
    return x * lax.rsqrt(ms + EPS) * g


def _inv_rms(x):
    return lax.rsqrt(jnp.mean(x * x, axis=-1, keepdims=True) + EPS)


def _swiglu(xn, wgu_ref, wd_ref, row_scale=None):
    acc = None
    for lo, hi in zip(FF_SPLITS[:-1], FF_SPLITS[1:]):
        gate = _dot(xn, wgu_ref[:, lo:hi])
        up = _dot(xn, wgu_ref[:, D_FF + lo:D_FF + hi])
        if row_scale is not None:
            gate, up = gate * row_scale, up * row_scale
        act = (gate * jax.nn.sigmoid(gate) * up).astype(BF16)
        part = _dot(act, wd_ref[lo:hi, :])
        acc = part if acc is None else acc + part
    return acc


def _ffn1_proj_kernel(x_ref, g1_ref, wgu_ref, wd_ref, gmix_ref, win_ref, gsum_ref, gqk_ref,
                      x1_ref, q_ref, k_ref, v_ref, u_ref):
    x = x_ref[...]
    x1 = x + 0.5 * _swiglu((x * g1_ref[...]).astype(BF16), wgu_ref, wd_ref, _inv_rms(x))
    x1_ref[...] = x1
    z = _dot((x1 * gmix_ref[...]).astype(BF16), win_ref[...]) * _inv_rms(x1)
    qk = z[:, :QK_WIDTH]
    sq = (qk * qk).astype(BF16)
    ss = jnp.concatenate(
        [_dot(sq[:, lo:lo + NORM_GROUP], gsum_ref[...]) for lo in range(0, ATT_WIDTH, NORM_GROUP)]
        + [_dot(sq[:, ATT_WIDTH:], gsum_ref[:KV_WIDTH, :KV_WIDTH])], axis=1)
    qkn = qk * lax.rsqrt(ss * (1.0 / HEAD_DIM) + EPS) * gqk_ref[...]
    q_ref[...] = qkn[:, :ATT_WIDTH].astype(BF16)
    k_ref[...] = qkn[:, ATT_WIDTH:]
    v_ref[...] = z[:, QK_WIDTH:QK_WIDTH + KV_WIDTH]
    a = z[:, QK_WIDTH + KV_WIDTH:QK_WIDTH + KV_WIDTH + CONV_WIDTH]
    g = z[:, QK_WIDTH + KV_WIDTH + CONV_WIDTH:]
    u = a * jax.nn.sigmoid(g)
    for grp in range(CONV_GROUPS):
        u_ref[pl.ds(grp, u.shape[0], stride=CONV_GROUPS), :] = u[:, grp * LANES:(grp + 1) * LANES]


def _out_ffn2_kernel(x1_ref, mix_ref, wout_ref, g2_ref, wgu_ref, wd_ref, gfin_ref, y_ref):
    x2 = x1_ref[...] + _dot(mix_ref[...], wout_ref[...])
    x3 = x2 + 0.5 * _swiglu((x2 * g2_ref[...]).astype(BF16), wgu_ref, wd_ref, _inv_rms(x2))
    y_ref[...] = _rms(x3, gfin_ref[...])


def _zero_of(x):
    bits = pltpu.bitcast(x, jnp.uint32)
    half = jnp.uint32(16)
    return lax.shift_right_logical(lax.shift_right_logical(bits, half), half)


def _after(x, zeros):
    packed = 2 * SUBLANES
    top = pltpu.bitcast(x[0:packed, :], jnp.uint32)
    top = top | jnp.concatenate([zeros] * (x.shape[1] // LANES), axis=1)
    return jnp.concatenate([pltpu.bitcast(top, x.dtype), x[packed:, :]], axis=0)


def _mixer_kernel(*refs, chunk, tile):
    stage, attend, conv = _mixer_parts(pl.program_id(1), *refs, chunk=chunk, piece=chunk, tile=tile,
                                       first_tile_has_no_past=False)
    stage()
    scores, values = attend
    for p in range(tile // chunk):
        values(p, scores(p)[0])
    shift, conv_block, n_blocks = conv
    shift()
    for r in range(n_blocks):
        conv_block(r)


def _mixer_ffn2_kernel(q_ref, kc_ref, kp_ref, vc_ref, vp_ref, uc_ref, up_ref, bias_ref, sink_ref,
                       cw_ref, cb_ref, lng_ref, lnb_ref,
                       x1_ref, wout_ref, g2_ref, wgu_ref, wd_ref, gfin_ref,
                       y_ref, mix_ref, x2_ref, xn_ref, xg_ref, act_ref, kw_ref, vw_ref, ue_ref, us_ref, ys_ref,
                       *, chunk, piece, tile, tiles_per_stream):
    s = pl.program_id(0)

    @pl.when(s == 0)
    def _():
        mix_ref[...] = jnp.zeros_like(mix_ref)

    t = jnp.minimum(s, pl.num_programs(0) - 2)
    stage, attend, conv = _mixer_parts(
        lax.rem(t, tiles_per_stream), q_ref, kc_ref, kp_ref, vc_ref, vp_ref, uc_ref, up_ref, bias_ref,
        sink_ref, cw_ref, cb_ref, lng_ref, lnb_ref, mix_ref, kw_ref, vw_ref, ue_ref, us_ref, ys_ref,
        chunk=chunk, piece=piece, tile=tile, first_tile_has_no_past=True)
    cols = lambda n: slice(n * MXU_TILE, (n + 1) * MXU_TILE)

    n_units, n_out = D_FF // MXU_TILE, D_MODEL // MXU_TILE

    def gate_up(n):
        lo = n * MXU_TILE
        if n == 0:
            gate = _dot(xg_ref[...], wgu_ref[:, lo:lo + MXU_TILE]) * inv_rms
            up = _dot(xg_ref[...], wgu_ref[:, D_FF + lo:D_FF + lo + MXU_TILE]) * inv_rms
        else:
            gate = _dot(xn_ref[...], wgu_ref[:, lo:lo + MXU_TILE])
            up = _dot(xn_ref[...], wgu_ref[:, D_FF + lo:D_FF + lo + MXU_TILE])
        act_ref[:, cols(n)] = (gate * jax.nn.sigmoid(gate) * up).astype(BF16)

    def down(n):
        x2_ref[:, cols(n)] = x2_ref[:, cols(n)] + 0.5 * _dot(act_ref[...], wd_ref[:, cols(n)])

    for n in range(n_out):
        x2_ref[:, cols(n)] = x1_ref[:, cols(n)] + _dot(mix_ref[...], wout_ref[:, cols(n)])
    stage()
    xg_ref[...] = (x2_ref[...] * g2_ref[...]).astype(BF16)
    inv_rms = _inv_rms(x2_ref[...])
    xn_ref[...] = (x2_ref[...] * inv_rms * g2_ref[...]).astype(BF16)
    scores, values = attend
    units = ([functools.partial(gate_up, n) for n in range(n_units)]
             + [functools.partial(down, n) for n in range(n_out)])
    shift, conv_block, n_blocks = conv
    shift()
    hosts = list(range(len(units) - 1)) + list(range(n_units, len(units) - 1))
    n_pieces = tile // piece
    assert n_pieces <= len(hosts)
    starts = [hosts[:n_pieces].count(n) for n in range(len(units))]
    pieces, blocks = iter(range(n_pieces)), iter(range(n_blocks))
    pending = []
    for n, unit in enumerate(units):
        unit()
        for p, prob in pending:
            values(p, prob)
        pending = []
        for p in itertools.islice(pieces, starts[n]):
            prob, zeros = scores(p)
            for r in itertools.islice(blocks, BLOCKS_PER_PIECE):
                zeros = conv_block(r, zeros)
            pending.append((p, _after(prob, zeros)))
    assert not pending
    for r in blocks:
        conv_block(r)
    y_ref[...] = _rms(x2_ref[...], gfin_ref[...])


def _mixer_parts(i, q_ref, kc_ref, kp_ref, vc_ref, vp_ref, uc_ref, up_ref, bias_ref, sink_ref,
                 cw_ref, cb_ref, lng_ref, lnb_ref, mix_ref, kw_ref, vw_ref, ue_ref, us_ref, ys_ref,
                 *, chunk, piece, tile, first_tile_has_no_past):
    kwin_len = WINDOW + chunk
    n_bias = bias_ref.shape[0]
    per_chunk = chunk // piece

    def scores(p):
        c, part = divmod(p, per_chunk)
        low_half = lax.broadcasted_iota(jnp.int32, (piece, LANES), 1) < HEAD_DIM
        sink = sink_ref[...]
        qc = q_ref[p * piece:(p + 1) * piece, :]
        zero = jnp.zeros_like(qc[:, :LANES])
        blocks = []
        for j in range(N_KV_HEADS):
            keep = low_half if j == 0 else jnp.logical_not(low_half)
            for g in range(GROUP):
                blocks.append(jnp.where(keep, qc[:, g * LANES:(g + 1) * LANES], zero))
        q_big = jnp.concatenate(blocks, axis=0)
        kwin = kw_ref[c * chunk:c * chunk + kwin_len, :]
        s = lax.dot_general(kwin, q_big, (((1,), (1,)), ((), ())), preferred_element_type=F32)
        bidx = jnp.minimum(i * (tile // chunk) + c, n_bias - 1)
        sb = s + bias_ref[bidx, part]
        m = jnp.maximum(jnp.max(sb, axis=0, keepdims=True), sink)
        e = jnp.exp(sb - m)
        denom = jnp.sum(e, axis=0, keepdims=True) + jnp.exp(sink - m)
        return (e * (1.0 / denom)).astype(BF16), _zero_of(s[0:SUBLANES, 0:LANES])

    def values(p, prob):
        c = p // per_chunk
        low_half = lax.broadcasted_iota(jnp.int32, (piece, LANES), 1) < HEAD_DIM
        vwin = vw_ref[c * chunk:c * chunk + kwin_len, :]
        o = lax.dot_general(prob, vwin, (((0,), (0,)), ((), ())), preferred_element_type=F32)
        for g in range(GROUP):
            o0 = o[g * piece:(g + 1) * piece, :]
            o1 = o[(GROUP + g) * piece:(GROUP + g + 1) * piece, :]
            mix_ref[p * piece:(p + 1) * piece, g * LANES:(g + 1) * LANES] = (
                jnp.where(low_half, o0, o1).astype(BF16))

    def stage():
        kw_ref[0:WINDOW, :] = kp_ref[...].astype(BF16)
        kw_ref[WINDOW:, :] = kc_ref[...].astype(BF16)
        vw_ref[0:WINDOW, :] = vp_ref[...].astype(BF16)
        vw_ref[WINDOW:, :] = vc_ref[...].astype(BF16)
        u_prev = up_ref[...]
        if first_tile_has_no_past:
            u_prev = jnp.where(i > 0, u_prev, 0.0)
        past_rows = CONV_GROUPS * CONV_PAD
        ue_ref[0:past_rows, :] = u_prev
        ue_ref[past_rows:, :] = uc_ref[...]

    def shift():
        us_ref[...] = ue_ref[CONV_GROUPS:CONV_GROUPS + us_ref.shape[0], :]

    rows = min(CONV_ROWS, tile)
    block_rows = CONV_GROUPS * rows
    first_tap = CONV_PAD - (CONV_K - 1)

    def conv_block(r, start=None):
        if start is None:
            acc = jnp.zeros((block_rows, LANES), F32)
        else:
            acc = pltpu.bitcast(jnp.concatenate([start] * (block_rows // SUBLANES), axis=0), F32)
        for j in range(CONV_K):
            lo = r * block_rows + CONV_GROUPS * (first_tap + j)
            if lo % SUBLANES == 0:
                taps = ue_ref[lo:lo + block_rows, :]
            else:
                taps = us_ref[lo - CONV_GROUPS:lo - CONV_GROUPS + block_rows, :]
            w = cw_ref[j * SUBLANES:(j + 1) * SUBLANES, :]
            acc = acc + taps * jnp.concatenate([w] * (block_rows // SUBLANES), axis=0)
        ys_ref[...] = acc + jnp.concatenate([cb_ref[...]] * (block_rows // SUBLANES), axis=0)
        y = jnp.concatenate([ys_ref[pl.ds(grp, rows, stride=CONV_GROUPS), :]
                             for grp in range(CONV_GROUPS)], axis=1)
        mu = jnp.mean(y, axis=-1, keepdims=True)
        yc = y - mu
        var = jnp.mean(yc * yc, axis=-1, keepdims=True)
        yn = yc * lax.rsqrt(var + EPS) * lng_ref[...] + lnb_ref[...]
        out = (yn * jax.nn.sigmoid(yn)).astype(BF16)
        mix_ref[r * rows:(r + 1) * rows, ATT_WIDTH:] = out
        packed = 2 * SUBLANES
        zeros = _zero_of(out[0:packed, 0:LANES])
        for lo in range(packed, rows, packed):
            zeros = zeros | _zero_of(out[lo:lo + packed, 0:LANES])
        return zeros

    return stage, (scores, values), (shift, conv_block, tile // rows)


def _const_spec(shape):
    return pl.BlockSpec(shape, lambda *_: (0,) * len(shape), pipeline_mode=pl.Buffered(1))


def _params(n_axes):
    return pltpu.CompilerParams(dimension_semantics=("arbitrary",) * n_axes,
                                vmem_limit_bytes=VMEM_LIMIT)


def _ffn1_proj(x, g1, wgu, wd, gmix, win, gsum, gqk):
    n = x.shape[0]
    tm = min(TOKEN_TILE, n)
    row = lambda w: pl.BlockSpec((tm, w), lambda i: (i, 0))
    return pl.pallas_call(
        _ffn1_proj_kernel,
        grid=(n // tm,),
        in_specs=[row(D_MODEL), _const_spec(g1.shape), _const_spec(wgu.shape), _const_spec(wd.shape),
                  _const_spec(gmix.shape), _const_spec(win.shape), _const_spec(gsum.shape),
                  _const_spec(gqk.shape)],
        out_specs=[row(D_MODEL), row(ATT_WIDTH), row(KV_WIDTH), row(KV_WIDTH),
                   pl.BlockSpec((CONV_GROUPS * tm, LANES), lambda i: (i, 0))],
        out_shape=[jax.ShapeDtypeStruct((n, D_MODEL), F32), jax.ShapeDtypeStruct((n, ATT_WIDTH), BF16),
                   jax.ShapeDtypeStruct((n, KV_WIDTH), F32), jax.ShapeDtypeStruct((n, KV_WIDTH), F32),
                   jax.ShapeDtypeStruct((CONV_GROUPS * n, LANES), F32)],
        compiler_params=_params(1),
        name="ffn1_proj",
    )(x, g1, wgu, wd, gmix, win, gsum, gqk)


def _out_ffn2(x1, mix, wout, g2, wgu, wd, gfin):
    n = x1.shape[0]
    tm = min(TOKEN_TILE, n)
    row = lambda w: pl.BlockSpec((tm, w), lambda i: (i, 0))
    return pl.pallas_call(
        _out_ffn2_kernel,
        grid=(n // tm,),
        in_specs=[row(D_MODEL), row(D_MODEL), _const_spec(wout.shape), _const_spec(g2.shape),
                  _const_spec(wgu.shape), _const_spec(wd.shape), _const_spec(gfin.shape)],
        out_specs=row(D_MODEL),
        out_shape=jax.ShapeDtypeStruct((n, D_MODEL), F32),
        compiler_params=_params(1),
        name="out_ffn2",
    )(x1, mix, wout, g2, wgu, wd, gfin)


def _mixer_scratch(tile):
    return [pltpu.VMEM((WINDOW + tile, KV_WIDTH), BF16),
            pltpu.VMEM((WINDOW + tile, KV_WIDTH), BF16),
            pltpu.VMEM((CONV_GROUPS * (CONV_PAD + tile), LANES), F32),
            pltpu.VMEM((CONV_GROUPS * (CONV_PAD + tile) - SUBLANES, LANES), F32),
            pltpu.VMEM((CONV_GROUPS * min(CONV_ROWS, tile), LANES), F32)]


def _mixer(q, k, v, u, k_past, v_past, u_past, bias, sink_row, cw, cb, lng, lnb,
           *, n_streams, seq, chunk, tile):
    nt = seq // tile
    cur = lambda w: pl.BlockSpec((tile, w), lambda b, i: (b * nt + i, 0))
    kv_past = pl.BlockSpec((WINDOW, KV_WIDTH), lambda b, i: (b, 0))
    u_past_spec = pl.BlockSpec((CONV_GROUPS * CONV_PAD, LANES), lambda b, i: (b, 0))
    body = functools.partial(_mixer_kernel, chunk=chunk, tile=tile)
    return pl.pallas_call(
        body,
        grid=(n_streams, nt),
        in_specs=[cur(ATT_WIDTH), cur(KV_WIDTH), kv_past, cur(KV_WIDTH), kv_past,
                  pl.BlockSpec((CONV_GROUPS * tile, LANES), lambda b, i: (b * nt + i, 0)),
                  u_past_spec, _const_spec(bias.shape), _const_spec(sink_row.shape),
                  _const_spec(cw.shape), _const_spec(cb.shape), _const_spec(lng.shape),
                  _const_spec(lnb.shape)],
        out_specs=cur(D_MODEL),
        out_shape=jax.ShapeDtypeStruct((n_streams * seq, D_MODEL), BF16),
        scratch_shapes=_mixer_scratch(tile),
        compiler_params=_params(2),
        name="mixer",
    )(q, k, k_past, v, v_past, u, u_past, bias, sink_row, cw, cb, lng, lnb)


def _mixer_ffn2(q, k, v, u, bias, sink_row, cw, cb, lng, lnb, x1, wout, g2, wgu, wd, gfin,
                *, seq, chunk, tile):
    n_tiles = q.shape[0] // tile
    mix_tile = lambda s: jnp.minimum(s, n_tiles - 1)
    ffn_tile = lambda s: jnp.maximum(s - 1, 0)
    cur = lambda w: pl.BlockSpec((tile, w), lambda s: (mix_tile(s), 0))
    kv_past = pl.BlockSpec(
        (WINDOW, KV_WIDTH), lambda s: (jnp.maximum(mix_tile(s) * (tile // WINDOW) - 1, 0), 0))
    u_past = pl.BlockSpec(
        (CONV_GROUPS * CONV_PAD, LANES),
        lambda s: (jnp.maximum(mix_tile(s) * (tile // CONV_PAD) - 1, 0), 0))
    ffn_row = pl.BlockSpec((tile, D_MODEL), lambda s: (ffn_tile(s), 0))
    body = functools.partial(_mixer_ffn2_kernel, chunk=chunk, piece=ATT_PIECE, tile=tile,
                             tiles_per_stream=seq // tile)
    return pl.pallas_call(
        body,
        grid=(n_tiles + 1,),
        in_specs=[cur(ATT_WIDTH), cur(KV_WIDTH), kv_past, cur(KV_WIDTH), kv_past,
                  pl.BlockSpec((CONV_GROUPS * tile, LANES), lambda s: (mix_tile(s), 0)),
                  u_past, _const_spec(bias.shape), _const_spec(sink_row.shape),
                  _const_spec(cw.shape), _const_spec(cb.shape), _const_spec(lng.shape),
                  _const_spec(lnb.shape),
                  ffn_row, _const_spec(wout.shape), _const_spec(g2.shape), _const_spec(wgu.shape),
                  _const_spec(wd.shape), _const_spec(gfin.shape)],
        out_specs=ffn_row,
        out_shape=jax.ShapeDtypeStruct((q.shape[0], D_MODEL), F32),
        scratch_shapes=[pltpu.VMEM((tile, D_MODEL), BF16), pltpu.VMEM((tile, D_MODEL), F32),
                        pltpu.VMEM((tile, D_MODEL), BF16), pltpu.VMEM((tile, D_MODEL), BF16),
                        pltpu.VMEM((tile, D_FF), BF16)]
        + _mixer_scratch(tile),
        compiler_params=_params(1),
        name="mixer_ffn2",
    )(q, k, k, v, v, u, u, bias, sink_row, cw, cb, lng, lnb, x1, wout, g2, wgu, wd, gfin)


def _rel_bucket(rel):
    nb = NUM_BUCKETS // 2
    max_exact = nb // 2
    ret = np.where(rel > 0, nb, 0)
    n = np.abs(rel)
    nf = np.maximum(n, 1).astype(np.float32)
    large = max_exact + (np.log(nf / np.float32(max_exact)) / np.float32(math.log(MAX_DISTANCE / max_exact))
                         * np.float32(nb - max_exact)).astype(np.int32)
    large = np.minimum(large, nb - 1)
    return (ret + np.where(n < max_exact, n, large)).astype(np.int32)


def _bias_cols(table, qpos, kpos, piece):
    bucket = _rel_bucket(kpos[:, None] - qpos[None, :])
    onehot = jnp.asarray(bucket[:, :, None] == np.arange(NUM_BUCKETS), dtype=F32)
    b = jnp.einsum("kqb,bh->khq", onehot, table.astype(F32), precision=lax.Precision.HIGHEST)
    b = b.reshape(kpos.shape[0], N_HEADS, qpos.shape[0] // piece, piece)
    return jnp.transpose(b, (2, 0, 1, 3)).reshape(-1, kpos.shape[0], N_HEADS * piece)


def kernel(x_prompt, x_sample, cache_k, cache_v, state_conv, rel_bias_table, ffn1_norm, ffn1_w_gu,
           ffn1_w_down, mix_norm, w_in, q_norm, k_norm, sinks, conv_w, conv_b, conv_ln_g, conv_ln_b,
           w_out, ffn2_norm, ffn2_w_gu, ffn2_w_down, final_norm):
    B, S, _ = x_prompt.shape
    DB, DS, _ = x_sample.shape
    R = cache_k.shape[2]
    past_len = R
    assert ffn1_norm.shape[0] == 1 and R == WINDOW and S % MIX_TILE == 0 and DS % 8 == 0
    l = 0
    row = lambda a: a.reshape(1, -1).astype(F32)

    pair = lambda a, axis: jnp.swapaxes(
        a.reshape(a.shape[:axis] + (N_KV_HEADS, GROUP, HEAD_DIM) + a.shape[axis + 1:]),
        axis, axis + 1).reshape(a.shape)
    win = jnp.concatenate([pair(w_in[l][:, :ATT_WIDTH], 1), w_in[l][:, ATT_WIDTH:]], axis=1).astype(BF16)
    wout = jnp.concatenate([pair(w_out[l][:ATT_WIDTH], 0), w_out[l][ATT_WIDTH:]], axis=0).astype(BF16)
    gqk = jnp.concatenate([jnp.tile(q_norm[l], N_HEADS) * (HEAD_DIM ** -0.5),
                           jnp.tile(k_norm[l], N_KV_HEADS)]).reshape(1, QK_WIDTH).astype(F32)
    head_of_col = np.arange(NORM_GROUP) // HEAD_DIM
    gsum = jnp.asarray(head_of_col[:, None] == head_of_col[None, :], dtype=BF16)
    wgu1, wd1 = ffn1_w_gu[l].astype(BF16), ffn1_w_down[l].astype(BF16)
    wgu2, wd2 = ffn2_w_gu[l].astype(BF16), ffn2_w_down[l].astype(BF16)
    by_group = lambda a: jnp.tile(a.astype(F32).reshape(-1, CONV_GROUPS, LANES),
                                  (1, SUBLANES // CONV_GROUPS, 1)).reshape(-1, LANES)
    cw, cb = by_group(conv_w[l]), by_group(conv_b[l])
    lng, lnb = row(conv_ln_g[l]), row(conv_ln_b[l])

    band = WINDOW + CHUNK
    bias_p = _bias_cols(rel_bias_table, np.arange(CHUNK), np.arange(band) - WINDOW, ATT_PIECE)
    key = np.arange(band)[None, :, None]
    neg = lambda first_valid: jnp.where(key >= first_valid, bias_p, -jnp.inf)
    bias_prompt = jnp.stack([neg(WINDOW), neg(WINDOW - CHUNK), bias_p])
    qpos_s = past_len + np.arange(DS)
    kpos_s = np.concatenate([past_len - R + np.arange(R), qpos_s])
    bias_sample = _bias_cols(rel_bias_table, qpos_s, kpos_s, DS)[None]
    sink_row = lambda n: jnp.broadcast_to(sinks[l].astype(F32)[:, None], (N_HEADS, n)).reshape(1, N_HEADS * n)

    def layer(x, n_streams, seq, chunk, tile, past):
        x1, q, k, v, u = _ffn1_proj(x, row(ffn1_norm[l]), wgu1, wd1, row(mix_norm[l]), win, gsum, gqk)
        if past is None:
            y = _mixer_ffn2(q, k, v, u, bias_prompt, sink_row(ATT_PIECE), cw, cb, lng, lnb, x1, wout,
                            row(ffn2_norm[l]), wgu2, wd2, row(final_norm[l]),
                            seq=seq, chunk=chunk, tile=tile)
        else:
            mix = _mixer(q, k, v, u, *past, bias_sample, sink_row(chunk), cw, cb, lng, lnb,
                         n_streams=n_streams, seq=seq, chunk=chunk, tile=tile)
            y = _out_ffn2(x1, mix, wout, row(ffn2_norm[l]), wgu2, wd2, row(final_norm[l]))
        return y, k, v, u

    yp, kp, vp, up = layer(x_prompt.reshape(B * S, D_MODEL), B, S, CHUNK, MIX_TILE, None)
    ck = cache_k[l].reshape(DB * R, KV_WIDTH).astype(F32)
    cv = cache_v[l].reshape(DB * R, KV_WIDTH).astype(F32)
    sc = jnp.pad(state_conv[l].astype(F32), ((0, 0), (CONV_PAD - (CONV_K - 1), 0), (0, 0)))
    ys, ks, vs, us = layer(x_sample.reshape(DB * DS, D_MODEL), DB, DS, DS, DS,
                           (ck, cv, sc.reshape(DB * CONV_PAD * CONV_GROUPS, LANES)))

    RP = min(WINDOW, S)
    heads = lambda a, n, t: a.reshape(n, t, N_KV_HEADS, HEAD_DIM)
    tail = lambda a, n: a.reshape(B, -1, a.shape[-1])[:, -n * (a.shape[0] // (B * S)):]
    new_k_prompt = heads(tail(kp, RP), B, RP)[None]
    new_v_prompt = heads(tail(vp, RP), B, RP)[None]
    new_conv_prompt = tail(up, CONV_K - 1).reshape(B, CONV_K - 1, CONV_WIDTH)[None]
    new_k_sample = jnp.concatenate([cache_k[l].astype(F32), heads(ks, DB, DS)], axis=1)[:, DS:][None]
    new_v_sample = jnp.concatenate([cache_v[l].astype(F32), heads(vs, DB, DS)], axis=1)[:, DS:][None]
    new_conv_sample = jnp.concatenate([state_conv[l].astype(F32), us.reshape(DB, DS, CONV_WIDTH)],
                                      axis=1)[:, DS:][None]
    return (yp.reshape(B, S, D_MODEL), ys.reshape(DB, DS, D_MODEL), new_k_prompt, new_v_prompt,
            new_conv_prompt, new_k_sample, new_v_sample, new_conv_sample)
```

```python
import functools
import itertools
import math

import numpy as np
import jax
import jax.numpy as jnp
from jax import lax
from jax.experimental import pallas as pl
from jax.experimental.pallas import tpu as pltpu

D_MODEL = 1024
CHUNK = 64
HEAD_DIM = 64
N_HEADS = 8
N_KV_HEADS = 2
GROUP = N_HEADS // N_KV_HEADS
ATT_WIDTH = N_HEADS * HEAD_DIM
KV_WIDTH = N_KV_HEADS * HEAD_DIM
CONV_WIDTH = D_MODEL - ATT_WIDTH
CONV_K = 31
WINDOW = 128
IN_WIDTH = ATT_WIDTH + 2 * KV_WIDTH + 2 * CONV_WIDTH
QK_WIDTH = ATT_WIDTH + KV_WIDTH
D_FF = 2816
NUM_BUCKETS = 32
MAX_DISTANCE = 128
EPS = 1e-6

LANES = 128
CONV_PAD = 32
FF_SPLITS = (0, 1536, D_FF)
SUBLANES = 8
MXU_TILE = 256
NORM_GROUP = MXU_TILE
CONV_ROWS = 16
CONV_GROUPS = CONV_WIDTH // LANES
TOKEN_TILE = 1024
MIX_TILE = 512
ATT_PIECE = 64
BLOCKS_PER_PIECE = 1
VMEM_LIMIT = 56 * 1024 * 1024

F32 = jnp.float32
BF16 = jnp.bfloat16


def _dot(a, b):
    return jnp.dot(a, b, preferred_element_type=F32)


def _rms(x, g):
    ms = jnp.mean(x * x, axis=-1, keepdims=True)
    return x * lax.rsqrt(ms + EPS) * g


def _inv_rms(x):
    return lax.rsqrt(jnp.mean(x * x, axis=-1, keepdims=True) + EPS)


def _swiglu(xn, wgu_ref, wd_ref, row_scale=None):
    acc = None
    for lo, hi in zip(FF_SPLITS[:-1], FF_SPLITS[1:]):
        gate = _dot(xn, wgu_ref[:, lo:hi])
        up = _dot(xn, wgu_ref[:, D_FF + lo:D_FF + hi])
        if row_scale is not None:
            gate, up = gate * row_scale, up * row_scale
        act = (gate * jax.nn.sigmoid(gate) * up).astype(BF16)
        part = _dot(act, wd_ref[lo:hi, :])
        acc = part if acc is None else acc + part
    return acc


def _ffn1_proj_kernel(x_ref, g1_ref, wgu_ref, wd_ref, gmix_ref, win_ref, gsum_ref, gqk_ref,
                      x1_ref, q_ref, k_ref, v_ref, u_ref):
    x = x_ref[...]
    x1 = x + 0.5 * _swiglu((x * g1_ref[...]).astype(BF16), wgu_ref, wd_ref, _inv_rms(x))
    x1_ref[...] = x1
    z = _dot((x1 * gmix_ref[...]).astype(BF16), win_ref[...]) * _inv_rms(x1)
    qk = z[:, :QK_WIDTH]
    sq = (qk * qk).astype(BF16)
    ss = jnp.concatenate(
        [_dot(sq[:, lo:lo + NORM_GROUP], gsum_ref[...]) for lo in range(0, ATT_WIDTH, NORM_GROUP)]
        + [_dot(sq[:, ATT_WIDTH:], gsum_ref[:KV_WIDTH, :KV_WIDTH])], axis=1)
    qkn = qk * lax.rsqrt(ss * (1.0 / HEAD_DIM) + EPS) * gqk_ref[...]
    q_ref[...] = qkn[:, :ATT_WIDTH].astype(BF16)
    k_ref[...] = qkn[:, ATT_WIDTH:]
    v_ref[...] = z[:, QK_WIDTH:QK_WIDTH + KV_WIDTH]
    a = z[:, QK_WIDTH + KV_WIDTH:QK_WIDTH + KV_WIDTH + CONV_WIDTH]
    g = z[:, QK_WIDTH + KV_WIDTH + CONV_WIDTH:]
    u = a * jax.nn.sigmoid(g)
    for grp in range(CONV_GROUPS):
        u_ref[pl.ds(grp, u.shape[0], stride=CONV_GROUPS), :] = u[:, grp * LANES:(grp + 1) * LANES]


def _out_ffn2_kernel(x1_ref, mix_ref, wout_ref, g2_ref, wgu_ref, wd_ref, gfin_ref, y_ref):
    x2 = x1_ref[...] + _dot(mix_ref[...], wout_ref[...])
    x3 = x2 + 0.5 * _swiglu((x2 * g2_ref[...]).astype(BF16), wgu_ref, wd_ref, _inv_rms(x2))
    y_ref[...] = _rms(x3, gfin_ref[...])


def _zero_of(x):
    bits = pltpu.bitcast(x, jnp.uint32)
    half = jnp.uint32(16)
    return lax.shift_right_logical(lax.shift_right_logical(bits, half), half)


def _after(x, zeros):
    packed = 2 * SUBLANES
    top = pltpu.bitcast(x[0:packed, :], jnp.uint32)
    top = top | jnp.concatenate([zeros] * (x.shape[1] // LANES), axis=1)
    return jnp.concatenate([pltpu.bitcast(top, x.dtype), x[packed:, :]], axis=0)


def _mixer_kernel(*refs, chunk, tile):
    stage, attend, conv = _mixer_parts(pl.program_id(1), *refs, chunk=chunk, piece=chunk, tile=tile,
                                       first_tile_has_no_past=False)
    stage()
    scores, values = attend
    for p in range(tile // chunk):
        values(p, scores(p)[0])
    shift, conv_block, n_blocks = conv
    shift()
    for r in range(n_blocks):
        conv_block(r)


def _mixer_ffn2_kernel(q_ref, kc_ref, kp_ref, vc_ref, vp_ref, uc_ref, up_ref, bias_ref, sink_ref,
                       cw_ref, cb_ref, lng_ref, lnb_ref,
                       x1_ref, wout_ref, g2_ref, wgu_ref, wd_ref, gfin_ref,
                       y_ref, mix_ref, x2_ref, xn_ref, xg_ref, act_ref, kw_ref, vw_ref, ue_ref, us_ref, ys_ref,
                       *, chunk, piece, tile, tiles_per_stream):
    s = pl.program_id(0)

    @pl.when(s == 0)
    def _():
        mix_ref[...] = jnp.zeros_like(mix_ref)

    t = jnp.minimum(s, pl.num_programs(0) - 2)
    stage, attend, conv = _mixer_parts(
        lax.rem(t, tiles_per_stream), q_ref, kc_ref, kp_ref, vc_ref, vp_ref, uc_ref, up_ref, bias_ref,
        sink_ref, cw_ref, cb_ref, lng_ref, lnb_ref, mix_ref, kw_ref, vw_ref, ue_ref, us_ref, ys_ref,
        chunk=chunk, piece=piece, tile=tile, first_tile_has_no_past=True)
    cols = lambda n: slice(n * MXU_TILE, (n + 1) * MXU_TILE)

    n_units, n_out = D_FF // MXU_TILE, D_MODEL // MXU_TILE

    def gate_up(n):
        lo = n * MXU_TILE
        if n == 0:
            gate = _dot(xg_ref[...], wgu_ref[:, lo:lo + MXU_TILE]) * inv_rms
            up = _dot(xg_ref[...], wgu_ref[:, D_FF + lo:D_FF + lo + MXU_TILE]) * inv_rms
        else:
            gate = _dot(xn_ref[...], wgu_ref[:, lo:lo + MXU_TILE])
            up = _dot(xn_ref[...], wgu_ref[:, D_FF + lo:D_FF + lo + MXU_TILE])
        act_ref[:, cols(n)] = (gate * jax.nn.sigmoid(gate) * up).astype(BF16)

    def down(n):
        x2_ref[:, cols(n)] = x2_ref[:, cols(n)] + 0.5 * _dot(act_ref[...], wd_ref[:, cols(n)])

    for n in range(n_out):
        x2_ref[:, cols(n)] = x1_ref[:, cols(n)] + _dot(mix_ref[...], wout_ref[:, cols(n)])
    stage()
    xg_ref[...] = (x2_ref[...] * g2_ref[...]).astype(BF16)
    inv_rms = _inv_rms(x2_ref[...])
    xn_ref[...] = (x2_ref[...] * inv_rms * g2_ref[...]).astype(BF16)
    scores, values = attend
    units = ([functools.partial(gate_up, n) for n in range(n_units)]
             + [functools.partial(down, n) for n in range(n_out)])
    shift, conv_block, n_blocks = conv
    shift()
    hosts = list(range(len(units) - 1)) + list(range(n_units, len(units) - 1))
    n_pieces = tile // piece
    assert n_pieces <= len(hosts)
    starts = [hosts[:n_pieces].count(n) for n in range(len(units))]
    pieces, blocks = iter(range(n_pieces)), iter(range(n_blocks))
    pending = []
    for n, unit in enumerate(units):
        unit()
        for p, prob in pending:
            values(p, prob)
        pending = []
        for p in itertools.islice(pieces, starts[n]):
            prob, zeros = scores(p)
            for r in itertools.islice(blocks, BLOCKS_PER_PIECE):
                zeros = conv_block(r, zeros)
            pending.append((p, _after(prob, zeros)))
    assert not pending
    for r in blocks:
        conv_block(r)
    y_ref[...] = _rms(x2_ref[...], gfin_ref[...])


def _mixer_parts(i, q_ref, kc_ref, kp_ref, vc_ref, vp_ref, uc_ref, up_ref, bias_ref, sink_ref,
                 cw_ref, cb_ref, lng_ref, lnb_ref, mix_ref, kw_ref, vw_ref, ue_ref, us_ref, ys_ref,
                 *, chunk, piece, tile, first_tile_has_no_past):
    kwin_len = WINDOW + chunk
    n_bias = bias_ref.shape[0]
    per_chunk = chunk // piece

    def scores(p):
        c, part = divmod(p, per_chunk)
        low_half = lax.broadcasted_iota(jnp.int32, (piece, LANES), 1) < HEAD_DIM
        sink = sink_ref[...]
        qc = q_ref[p * piece:(p + 1) * piece, :]
        zero = jnp.zeros_like(qc[:, :LANES])
        blocks = []
        for j in range(N_KV_HEADS):
            keep = low_half if j == 0 else jnp.logical_not(low_half)
            for g in range(GROUP):
                blocks.append(jnp.where(keep, qc[:, g * LANES:(g + 1) * LANES], zero))
        q_big = jnp.concatenate(blocks, axis=0)
        kwin = kw_ref[c * chunk:c * chunk + kwin_len, :]
        s = lax.dot_general(kwin, q_big, (((1,), (1,)), ((), ())), preferred_element_type=F32)
        bidx = jnp.minimum(i * (tile // chunk) + c, n_bias - 1)
        sb = s + bias_ref[bidx, part]
        m = jnp.maximum(jnp.max(sb, axis=0, keepdims=True), sink)
        e = jnp.exp(sb - m)
        denom = jnp.sum(e, axis=0, keepdims=True) + jnp.exp(sink - m)
        return (e * (1.0 / denom)).astype(BF16), _zero_of(s[0:SUBLANES, 0:LANES])

    def values(p, prob):
        c = p // per_chunk
        low_half = lax.broadcasted_iota(jnp.int32, (piece, LANES), 1) < HEAD_DIM
        vwin = vw_ref[c * chunk:c * chunk + kwin_len, :]
        o = lax.dot_general(prob, vwin, (((0,), (0,)), ((), ())), preferred_element_type=F32)
        for g in range(GROUP):
            o0 = o[g * piece:(g + 1) * piece, :]
            o1 = o[(GROUP + g) * piece:(GROUP + g + 1) * piece, :]
            mix_ref[p * piece:(p + 1) * piece, g * LANES:(g + 1) * LANES] = (
                jnp.where(low_half, o0, o1).astype(BF16))

    def stage():
        kw_ref[0:WINDOW, :] = kp_ref[...].astype(BF16)
        kw_ref[WINDOW:, :] = kc_ref[...].astype(BF16)
        vw_ref[0:WINDOW, :] = vp_ref[...].astype(BF16)
        vw_ref[WINDOW:, :] = vc_ref[...].astype(BF16)
        u_prev = up_ref[...]
        if first_tile_has_no_past:
            u_prev = jnp.where(i > 0, u_prev, 0.0)
        past_rows = CONV_GROUPS * CONV_PAD
        ue_ref[0:past_rows, :] = u_prev
        ue_ref[past_rows:, :] = uc_ref[...]

    def shift():
        n = ue_ref.shape[0] - SUBLANES
        us_ref[0:n, :] = ue_ref[CONV_GROUPS:CONV_GROUPS + n, :]

    rows = min(CONV_ROWS, tile)
    block_rows = CONV_GROUPS * rows
    first_tap = CONV_PAD - (CONV_K - 1)

    def conv_block(r, start=None):
        if start is None:
            acc = jnp.zeros((block_rows, LANES), F32)
        else:
            acc = pltpu.bitcast(jnp.concatenate([start] * (block_rows // SUBLANES), axis=0), F32)
        for j in range(CONV_K):
            lo = r * block_rows + CONV_GROUPS * (first_tap + j)
            if lo % SUBLANES == 0:
                taps = ue_ref[lo:lo + block_rows, :]
            else:
                taps = us_ref[lo - CONV_GROUPS:lo - CONV_GROUPS + block_rows, :]
            w = cw_ref[j * SUBLANES:(j + 1) * SUBLANES, :]
            acc = acc + taps * jnp.concatenate([w] * (block_rows // SUBLANES), axis=0)
        ys_ref[...] = acc + jnp.concatenate([cb_ref[...]] * (block_rows // SUBLANES), axis=0)
        y = jnp.concatenate([ys_ref[pl.ds(grp, rows, stride=CONV_GROUPS), :]
                             for grp in range(CONV_GROUPS)], axis=1)
        mu = jnp.mean(y, axis=-1, keepdims=True)
        yc = y - mu
        var = jnp.mean(yc * yc, axis=-1, keepdims=True)
        yn = yc * lax.rsqrt(var + EPS) * lng_ref[...] + lnb_ref[...]
        out = (yn * jax.nn.sigmoid(yn)).astype(BF16)
        mix_ref[r * rows:(r + 1) * rows, ATT_WIDTH:] = out
        packed = 2 * SUBLANES
        zeros = _zero_of(out[0:packed, 0:LANES])
        for lo in range(packed, rows, packed):
            zeros = zeros | _zero_of(out[lo:lo + packed, 0:LANES])
        return zeros

    return stage, (scores, values), (shift, conv_block, tile // rows)


def _const_spec(shape):
    return pl.BlockSpec(shape, lambda *_: (0,) * len(shape), pipeline_mode=pl.Buffered(1))


def _params(n_axes):
    return pltpu.CompilerParams(dimension_semantics=("arbitrary",) * n_axes,
                                vmem_limit_bytes=VMEM_LIMIT)


def _ffn1_proj(x, g1, wgu, wd, gmix, win, gsum, gqk):
    n = x.shape[0]
    tm = min(TOKEN_TILE, n)
    row = lambda w: pl.BlockSpec((tm, w), lambda i: (i, 0))
    return pl.pallas_call(
        _ffn1_proj_kernel,
        grid=(n // tm,),
        in_specs=[row(D_MODEL), _const_spec(g1.shape), _const_spec(wgu.shape), _const_spec(wd.shape),
                  _const_spec(gmix.shape), _const_spec(win.shape), _const_spec(gsum.shape),
                  _const_spec(gqk.shape)],
        out_specs=[row(D_MODEL), row(ATT_WIDTH), row(KV_WIDTH), row(KV_WIDTH),
                   pl.BlockSpec((CONV_GROUPS * tm, LANES), lambda i: (i, 0))],
        out_shape=[jax.ShapeDtypeStruct((n, D_MODEL), F32), jax.ShapeDtypeStruct((n, ATT_WIDTH), BF16),
                   jax.ShapeDtypeStruct((n, KV_WIDTH), F32), jax.ShapeDtypeStruct((n, KV_WIDTH), F32),
                   jax.ShapeDtypeStruct((CONV_GROUPS * n, LANES), F32)],
        compiler_params=_params(1),
        name="ffn1_proj",
    )(x, g1, wgu, wd, gmix, win, gsum, gqk)


def _out_ffn2(x1, mix, wout, g2, wgu, wd, gfin):
    n = x1.shape[0]
    tm = min(TOKEN_TILE, n)
    row = lambda w: pl.BlockSpec((tm, w), lambda i: (i, 0))
    return pl.pallas_call(
        _out_ffn2_kernel,
        grid=(n // tm,),
        in_specs=[row(D_MODEL), row(D_MODEL), _const_spec(wout.shape), _const_spec(g2.shape),
                  _const_spec(wgu.shape), _const_spec(wd.shape), _const_spec(gfin.shape)],
        out_specs=row(D_MODEL),
        out_shape=jax.ShapeDtypeStruct((n, D_MODEL), F32),
        compiler_params=_params(1),
        name="out_ffn2",
    )(x1, mix, wout, g2, wgu, wd, gfin)


def _mixer_scratch(tile):
    return [pltpu.VMEM((WINDOW + tile, KV_WIDTH), BF16),
            pltpu.VMEM((WINDOW + tile, KV_WIDTH), BF16),
            pltpu.VMEM((CONV_GROUPS * (CONV_PAD + tile), LANES), F32),
            pltpu.VMEM((CONV_GROUPS * (CONV_PAD + tile), LANES), F32),
            pltpu.VMEM((CONV_GROUPS * min(CONV_ROWS, tile), LANES), F32)]


def _mixer(q, k, v, u, k_past, v_past, u_past, bias, sink_row, cw, cb, lng, lnb,
           *, n_streams, seq, chunk, tile):
    nt = seq // tile
    cur = lambda w: pl.BlockSpec((tile, w), lambda b, i: (b * nt + i, 0))
    kv_past = pl.BlockSpec((WINDOW, KV_WIDTH), lambda b, i: (b, 0))
    u_past_spec = pl.BlockSpec((CONV_GROUPS * CONV_PAD, LANES), lambda b, i: (b, 0))
    body = functools.partial(_mixer_kernel, chunk=chunk, tile=tile)
    return pl.pallas_call(
        body,
        grid=(n_streams, nt),
        in_specs=[cur(ATT_WIDTH), cur(KV_WIDTH), kv_past, cur(KV_WIDTH), kv_past,
                  pl.BlockSpec((CONV_GROUPS * tile, LANES), lambda b, i: (b * nt + i, 0)),
                  u_past_spec, _const_spec(bias.shape), _const_spec(sink_row.shape),
                  _const_spec(cw.shape), _const_spec(cb.shape), _const_spec(lng.shape),
                  _const_spec(lnb.shape)],
        out_specs=cur(D_MODEL),
        out_shape=jax.ShapeDtypeStruct((n_streams * seq, D_MODEL), BF16),
        scratch_shapes=_mixer_scratch(tile),
        compiler_params=_params(2),
        name="mixer",
    )(q, k, k_past, v, v_past, u, u_past, bias, sink_row, cw, cb, lng, lnb)


def _mixer_ffn2(q, k, v, u, bias, sink_row, cw, cb, lng, lnb, x1, wout, g2, wgu, wd, gfin,
                *, seq, chunk, tile):
    n_tiles = q.shape[0] // tile
    mix_tile = lambda s: jnp.minimum(s, n_tiles - 1)
    ffn_tile = lambda s: jnp.maximum(s - 1, 0)
    cur = lambda w: pl.BlockSpec((tile, w), lambda s: (mix_tile(s), 0))
    kv_past = pl.BlockSpec(
        (WINDOW, KV_WIDTH), lambda s: (jnp.maximum(mix_tile(s) * (tile // WINDOW) - 1, 0), 0))
    u_past = pl.BlockSpec(
        (CONV_GROUPS * CONV_PAD, LANES),
        lambda s: (jnp.maximum(mix_tile(s) * (tile // CONV_PAD) - 1, 0), 0))
    ffn_row = pl.BlockSpec((tile, D_MODEL), lambda s: (ffn_tile(s), 0))
    body = functools.partial(_mixer_ffn2_kernel, chunk=chunk, piece=ATT_PIECE, tile=tile,
                             tiles_per_stream=seq // tile)
    return pl.pallas_call(
        body,
        grid=(n_tiles + 1,),
        in_specs=[cur(ATT_WIDTH), cur(KV_WIDTH), kv_past, cur(KV_WIDTH), kv_past,
                  pl.BlockSpec((CONV_GROUPS * tile, LANES), lambda s: (mix_tile(s), 0)),
                  u_past, _const_spec(bias.shape), _const_spec(sink_row.shape),
                  _const_spec(cw.shape), _const_spec(cb.shape), _const_spec(lng.shape),
                  _const_spec(lnb.shape),
                  ffn_row, _const_spec(wout.shape), _const_spec(g2.shape), _const_spec(wgu.shape),
                  _const_spec(wd.shape), _const_spec(gfin.shape)],
        out_specs=ffn_row,
        out_shape=jax.ShapeDtypeStruct((q.shape[0], D_MODEL), F32),
        scratch_shapes=[pltpu.VMEM((tile, D_MODEL), BF16), pltpu.VMEM((tile, D_MODEL), F32),
                        pltpu.VMEM((tile, D_MODEL), BF16), pltpu.VMEM((tile, D_MODEL), BF16),
                        pltpu.VMEM((tile, D_FF), BF16)]
        + _mixer_scratch(tile),
        compiler_params=_params(1),
        name="mixer_ffn2",
    )(q, k, k, v, v, u, u, bias, sink_row, cw, cb, lng, lnb, x1, wout, g2, wgu, wd, gfin)


def _rel_bucket(rel):
    nb = NUM_BUCKETS // 2
    max_exact = nb // 2
    ret = np.where(rel > 0, nb, 0)
    n = np.abs(rel)
    nf = np.maximum(n, 1).astype(np.float32)
    large = max_exact + (np.log(nf / np.float32(max_exact)) / np.float32(math.log(MAX_DISTANCE / max_exact))
                         * np.float32(nb - max_exact)).astype(np.int32)
    large = np.minimum(large, nb - 1)
    return (ret + np.where(n < max_exact, n, large)).astype(np.int32)


def _bias_cols(table, qpos, kpos, piece):
    bucket = _rel_bucket(kpos[:, None] - qpos[None, :])
    onehot = jnp.asarray(bucket[:, :, None] == np.arange(NUM_BUCKETS), dtype=F32)
    b = jnp.einsum("kqb,bh->khq", onehot, table.astype(F32), precision=lax.Precision.HIGHEST)
    b = b.reshape(kpos.shape[0], N_HEADS, qpos.shape[0] // piece, piece)
    return jnp.transpose(b, (2, 0, 1, 3)).reshape(-1, kpos.shape[0], N_HEADS * piece)


def kernel(x_prompt, x_sample, cache_k, cache_v, state_conv, rel_bias_table, ffn1_norm, ffn1_w_gu,
           ffn1_w_down, mix_norm, w_in, q_norm, k_norm, sinks, conv_w, conv_b, conv_ln_g, conv_ln_b,
           w_out, ffn2_norm, ffn2_w_gu, ffn2_w_down, final_norm):
    B, S, _ = x_prompt.shape
    DB, DS, _ = x_sample.shape
    R = cache_k.shape[2]
    past_len = R
    assert ffn1_norm.shape[0] == 1 and R == WINDOW and S % MIX_TILE == 0 and DS % 8 == 0
    l = 0
    row = lambda a: a.reshape(1, -1).astype(F32)

    pair = lambda a, axis: jnp.swapaxes(
        a.reshape(a.shape[:axis] + (N_KV_HEADS, GROUP, HEAD_DIM) + a.shape[axis + 1:]),
        axis, axis + 1).reshape(a.shape)
    win = jnp.concatenate([pair(w_in[l][:, :ATT_WIDTH], 1), w_in[l][:, ATT_WIDTH:]], axis=1).astype(BF16)
    wout = jnp.concatenate([pair(w_out[l][:ATT_WIDTH], 0), w_out[l][ATT_WIDTH:]], axis=0).astype(BF16)
    gqk = jnp.concatenate([jnp.tile(q_norm[l], N_HEADS) * (HEAD_DIM ** -0.5),
                           jnp.tile(k_norm[l], N_KV_HEADS)]).reshape(1, QK_WIDTH).astype(F32)
    head_of_col = np.arange(NORM_GROUP) // HEAD_DIM
    gsum = jnp.asarray(head_of_col[:, None] == head_of_col[None, :], dtype=BF16)
    wgu1, wd1 = ffn1_w_gu[l].astype(BF16), ffn1_w_down[l].astype(BF16)
    wgu2, wd2 = ffn2_w_gu[l].astype(BF16), ffn2_w_down[l].astype(BF16)
    by_group = lambda a: jnp.tile(a.astype(F32).reshape(-1, CONV_GROUPS, LANES),
                                  (1, SUBLANES // CONV_GROUPS, 1)).reshape(-1, LANES)
    cw, cb = by_group(conv_w[l]), by_group(conv_b[l])
    lng, lnb = row(conv_ln_g[l]), row(conv_ln_b[l])

    band = WINDOW + CHUNK
    bias_p = _bias_cols(rel_bias_table, np.arange(CHUNK), np.arange(band) - WINDOW, ATT_PIECE)
    key = np.arange(band)[None, :, None]
    neg = lambda first_valid: jnp.where(key >= first_valid, bias_p, -jnp.inf)
    bias_prompt = jnp.stack([neg(WINDOW), neg(WINDOW - CHUNK), bias_p])
    qpos_s = past_len + np.arange(DS)
    kpos_s = np.concatenate([past_len - R + np.arange(R), qpos_s])
    bias_sample = _bias_cols(rel_bias_table, qpos_s, kpos_s, DS)[None]
    sink_row = lambda n: jnp.broadcast_to(sinks[l].astype(F32)[:, None], (N_HEADS, n)).reshape(1, N_HEADS * n)

    def layer(x, n_streams, seq, chunk, tile, past):
        x1, q, k, v, u = _ffn1_proj(x, row(ffn1_norm[l]), wgu1, wd1, row(mix_norm[l]), win, gsum, gqk)
        if past is None:
            y = _mixer_ffn2(q, k, v, u, bias_prompt, sink_row(ATT_PIECE), cw, cb, lng, lnb, x1, wout,
                            row(ffn2_norm[l]), wgu2, wd2, row(final_norm[l]),
                            seq=seq, chunk=chunk, tile=tile)
        else:
            mix = _mixer(q, k, v, u, *past, bias_sample, sink_row(chunk), cw, cb, lng, lnb,
                         n_streams=n_streams, seq=seq, chunk=chunk, tile=tile)
            y = _out_ffn2(x1, mix, wout, row(ffn2_norm[l]), wgu2, wd2, row(final_norm[l]))
        return y, k, v, u

    yp, kp, vp, up = layer(x_prompt.reshape(B * S, D_MODEL), B, S, CHUNK, MIX_TILE, None)
    ck = cache_k[l].reshape(DB * R, KV_WIDTH).astype(F32)
    cv = cache_v[l].reshape(DB * R, KV_WIDTH).astype(F32)
    sc = jnp.pad(state_conv[l].astype(F32), ((0, 0), (CONV_PAD - (CONV_K - 1), 0), (0, 0)))
    ys, ks, vs, us = layer(x_sample.reshape(DB * DS, D_MODEL), DB, DS, DS, DS,
                           (ck, cv, sc.reshape(DB * CONV_PAD * CONV_GROUPS, LANES)))

    RP = min(WINDOW, S)
    heads = lambda a, n, t: a.reshape(n, t, N_KV_HEADS, HEAD_DIM)
    tail = lambda a, n: a.reshape(B, -1, a.shape[-1])[:, -n * (a.shape[0] // (B * S)):]
    new_k_prompt = heads(tail(kp, RP), B, RP)[None]
    new_v_prompt = heads(tail(vp, RP), B, RP)[None]
    new_conv_prompt = tail(up, CONV_K - 1).reshape(B, CONV_K - 1, CONV_WIDTH)[None]
    new_k_sample = jnp.concatenate([cache_k[l].astype(F32), heads(ks, DB, DS)], axis=1)[:, DS:][None]
    new_v_sample = jnp.concatenate([cache_v[l].astype(F32), heads(vs, DB, DS)], axis=1)[:, DS:][None]
    new_conv_sample = jnp.concatenate([state_conv[l].astype(F32), us.reshape(DB, DS, CONV_WIDTH)],
                                      axis=1)[:, DS:][None]
    return (yp.reshape(B, S, D_MODEL), ys.reshape(DB, DS, D_MODEL), new_k_prompt, new_v_prompt,
            new_conv_prompt, new_k_sample, new_v_sample, new_conv_sample)
```
